```python
import jax, jax.numpy as jnp
from jax import lax
import numpy as np

D_MODEL = 1024
BATCH = 32
SEQ = 2048
DEPTH = 1

CHUNK = 64
N_HEADS = 8
HEAD_DIM = 64
ATTN_WIDTH = N_HEADS * HEAD_DIM
N_IDX_HEADS = 4
IDX_DIM = 64
TOPK_MAX = 256
Q_BLOCK = 128
POOL_WINDOWS = (2, 4, 8, 16)
N_POOL_GROUPS = 4
POOL_WIDTH = 512
POOL_GROUP = POOL_WIDTH // N_POOL_GROUPS
N_BRANCHES = 2
PEER_HEADS = 8
PEER_QDIM = 128
PEER_HALF = PEER_QDIM // 2
N_KEYS = 128
N_EXPERTS = N_KEYS * N_KEYS
PEER_TOPK = 16
TOKEN_BLOCK = 128
EPS = 1e-6
IN_SIZES = (ATTN_WIDTH, HEAD_DIM, HEAD_DIM, N_IDX_HEADS * IDX_DIM, IDX_DIM, N_IDX_HEADS, POOL_WIDTH, N_BRANCHES * D_MODEL)
IN_WIDTH = sum(IN_SIZES)

kernel_name = 'chunk_causal_dsa_pool_peer_hybrid'


def rms_norm(x, g):
    x32 = x.astype(jnp.float32)
    y = x32 * lax.rsqrt(jnp.mean(x32 * x32, axis=-1, keepdims=True) + EPS)
    return (y * g.astype(jnp.float32)).astype(x.dtype)


def dsa_attention(q, k, v, iq, ik, iw):
    B, S = q.shape[0], q.shape[1]
    n_sel = min(TOPK_MAX, S // 4)
    n_blk = S // Q_BLOCK
    key_chunk = jnp.arange(S) // CHUNK
    idx_scale = IDX_DIM ** -0.5
    w_scale = N_IDX_HEADS ** -0.5
    attn_scale = HEAD_DIM ** -0.5

    def to_blocks(a):
        return jnp.swapaxes(a.reshape((B, n_blk, Q_BLOCK) + a.shape[2:]), 0, 1)

    def gather_rows(table, idx):
        return jax.vmap(lambda tb, ib: tb[ib])(table, idx)

    def one_block(args):
        qb, iqb, iwb, blk = args
        q_chunk = (blk * Q_BLOCK + jnp.arange(Q_BLOCK)) // CHUNK
        admissible = key_chunk[None, :] <= q_chunk[:, None]
        rel = jax.nn.relu(jnp.einsum('bqhd,bsd->bqhs', iqb, ik) * idx_scale)
        score = jnp.einsum('bqhs,bqh->bqs', rel, iwb * w_scale).astype(jnp.float32)
        score = jnp.where(admissible[None], score, -jnp.inf)
        _, sel = lax.top_k(score, n_sel)
        k_sel = gather_rows(k, sel)
        v_sel = gather_rows(v, sel)
        ok = key_chunk[sel] <= q_chunk[None, :, None]
        logits = jnp.einsum('bqhd,bqkd->bqhk', qb, k_sel).astype(jnp.float32) * attn_scale
        logits = jnp.where(ok[:, :, None, :], logits, -jnp.inf)
        p = jax.nn.softmax(logits, axis=-1).astype(v.dtype)
        return jnp.einsum('bqhk,bqkd->bqhd', p, v_sel)

    out = lax.map(one_block, (to_blocks(q), to_blocks(iq), to_blocks(iw), jnp.arange(n_blk)))
    return jnp.swapaxes(out, 0, 1).reshape(B, S, ATTN_WIDTH)


def pool_mixer(p, w_grp, s_pool):
    B, S = p.shape[0], p.shape[1]
    pg = p.reshape(B, S, N_POOL_GROUPS, POOL_GROUP).astype(jnp.float32)
    cs = jnp.pad(jnp.cumsum(pg, axis=1), ((0, 0), (1, 0), (0, 0), (0, 0)))
    t = jnp.arange(S)
    outs = []
    for g, w in enumerate(POOL_WINDOWS):
        cs_g = cs[:, :, g, :]
        lo = jnp.maximum(t + 1 - w, 0)
        cnt = (t + 1 - lo).astype(jnp.float32)
        mean = (cs_g[:, t + 1] - cs_g[:, lo]) / cnt[None, :, None]
        outs.append(mean - pg[:, :, g, :])
    mixed = jnp.stack(outs, axis=2).astype(p.dtype)
    mixed = jnp.einsum('bsgc,gcd->bsgd', mixed, w_grp) * s_pool
    return mixed.reshape(B, S, POOL_WIDTH)


def peer_ffn(h, w_q, sub_keys, u, v):
    B, S, D = h.shape
    n_blk = (B * S) // TOKEN_BLOCK
    hb = h.reshape(n_blk, TOKEN_BLOCK, D)

    def one_block(hx):
        q = (hx @ w_q).reshape(TOKEN_BLOCK, PEER_HEADS, 2, PEER_HALF)
        s = jnp.einsum('nhpd,hpkd->nhpk', q, sub_keys).astype(jnp.float32)
        top_s, top_i = lax.top_k(s, PEER_TOPK)
        cand = top_s[:, :, 0, :, None] + top_s[:, :, 1, None, :]
        best_s, best_p = lax.top_k(cand.reshape(TOKEN_BLOCK, PEER_HEADS, PEER_TOPK * PEER_TOPK), PEER_TOPK)
        i1 = jnp.take_along_axis(top_i[:, :, 0], best_p // PEER_TOPK, axis=-1)
        i2 = jnp.take_along_axis(top_i[:, :, 1], best_p % PEER_TOPK, axis=-1)
        expert = (i1 * N_KEYS + i2).reshape(TOKEN_BLOCK, PEER_HEADS * PEER_TOPK)
        gate = jax.nn.softmax(best_s, axis=-1).reshape(TOKEN_BLOCK, PEER_HEADS * PEER_TOPK).astype(hx.dtype)
        u_sel = u[expert]
        v_sel = v[expert]
        act = jax.nn.gelu(jnp.einsum('nd,ned->ne', hx, u_sel), approximate=False)
        return jnp.einsum('ne,ned->nd', gate * act, v_sel)

    return lax.map(one_block, hb).reshape(B, S, D)


def setup_inputs(seed: int = 0) -> dict:
    key = jax.random.key(seed)
    ks = jax.random.split(key, 20)
    f32 = jnp.float32
    D = D_MODEL

    def nrm(k, shape, scale):
        return jax.random.normal(k, shape, f32) * scale

    return {
        'x': nrm(ks[0], (BATCH, SEQ, D), 1.0),
        'c': nrm(ks[1], (BATCH, D), 1.0),
        'w_ada': nrm(ks[2], (DEPTH, D, 6 * D), 0.5 * D ** -0.5),
        'b_ada': nrm(ks[3], (DEPTH, 6 * D), 0.02),
        'g_norm1': 1.0 + nrm(ks[4], (DEPTH, D), 0.02),
        'w_in': nrm(ks[5], (DEPTH, D, IN_WIDTH), D ** -0.5),
        'g_q': 1.0 + nrm(ks[6], (DEPTH, HEAD_DIM), 0.02),
        'g_k': 1.0 + nrm(ks[7], (DEPTH, HEAD_DIM), 0.02),
        'g_ik': 1.0 + nrm(ks[8], (DEPTH, IDX_DIM), 0.02),
        'w_pool_grp': nrm(ks[9], (DEPTH, N_POOL_GROUPS, POOL_GROUP, POOL_GROUP), POOL_GROUP ** -0.5),
        's_pool': 1.0 + nrm(ks[10], (DEPTH, N_POOL_GROUPS, POOL_GROUP), 0.1),
        'w_up_attn': nrm(ks[11], (DEPTH, ATTN_WIDTH, D), ATTN_WIDTH ** -0.5),
        'w_up_pool': nrm(ks[12], (DEPTH, POOL_WIDTH, D), POOL_WIDTH ** -0.5),
        'w_out': nrm(ks[13], (DEPTH, D, D), D ** -0.5),
        'g_norm2': 1.0 + nrm(ks[14], (DEPTH, D), 0.02),
        'w_peer_q': nrm(ks[15], (DEPTH, D, PEER_HEADS * PEER_QDIM), D ** -0.5),
        'peer_subkeys': nrm(ks[16], (DEPTH, PEER_HEADS, 2, N_KEYS, PEER_HALF), PEER_HALF ** -0.5),
        'peer_u': nrm(ks[17], (DEPTH, N_EXPERTS, D), D ** -0.5),
        'peer_v': nrm(ks[18], (DEPTH, N_EXPERTS, D), 0.5),
    }


def reference(x, c, w_ada, b_ada, g_norm1, w_in, g_q, g_k, g_ik, w_pool_grp, s_pool,
              w_up_attn, w_up_pool, w_out, g_norm2, w_peer_q, peer_subkeys, peer_u, peer_v):
    B, S, D = x.shape
    cuts = [int(v) for v in np.cumsum(IN_SIZES)[:-1]]
    for layer in range(DEPTH):
        ada = c @ w_ada[layer] + b_ada[layer]
        shift1, scale1, gate1, shift2, scale2, gate2 = [a[:, None, :] for a in jnp.split(ada, 6, axis=-1)]

        h = rms_norm(x, g_norm1[layer]) * (1.0 + scale1) + shift1
        z = h @ w_in[layer]
        zq, zk, zv, ziq, zik, ziw, zp, zg = jnp.split(z, cuts, axis=-1)
        q = rms_norm(zq.reshape(B, S, N_HEADS, HEAD_DIM), g_q[layer])
        k = rms_norm(zk, g_k[layer])
        iq = ziq.reshape(B, S, N_IDX_HEADS, IDX_DIM)
        ik = rms_norm(zik, g_ik[layer])
        y_attn = dsa_attention(q, k, zv, iq, ik, ziw) @ w_up_attn[layer]
        y_pool = pool_mixer(zp, w_pool_grp[layer], s_pool[layer]) @ w_up_pool[layer]
        g_attn, g_pool = jnp.split(jax.nn.sigmoid(zg), 2, axis=-1)
        mixed = (g_attn * y_attn + g_pool * y_pool) @ w_out[layer]
        x = x + gate1 * mixed

        h2 = rms_norm(x, g_norm2[layer]) * (1.0 + scale2) + shift2
        x = x + gate2 * peer_ffn(h2, w_peer_q[layer], peer_subkeys[layer], peer_u[layer], peer_v[layer])
    return x
```

```python
import functools

import jax
import jax.numpy as jnp
import numpy as np
from jax import lax
from jax.experimental import pallas as pl
from jax.experimental.pallas import tpu as pltpu

D_MODEL = 1024
CHUNK = 64
N_HEADS = 8
HEAD_DIM = 64
ATTN_WIDTH = N_HEADS * HEAD_DIM
N_IDX_HEADS = 4
IDX_DIM = 64
TOPK_MAX = 256
POOL_WINDOWS = (2, 4, 8, 16)
N_POOL_GROUPS = 4
POOL_WIDTH = 512
POOL_GROUP = POOL_WIDTH // N_POOL_GROUPS
PEER_HEADS = 8
PEER_HALF = 64
N_KEYS = 128
PEER_TOPK = 16
EPS = 1e-6
POOL_HALO = 16

_Q0, _K0, _V0, _IQ0, _IK0, _IW0 = 0, 512, 576, 640, 896, 960
_SLAB_A = 1024
_ZP0 = 964
_ZG0 = 1476

MXU_DTYPE = jnp.bfloat16
VMEM_LIMIT = 56 * 1024 * 1024

_INT_MIN = -(2 ** 31)
_HI = lax.Precision.HIGHEST


def _cparams(sem):
    return pltpu.CompilerParams(dimension_semantics=sem, vmem_limit_bytes=VMEM_LIMIT)


def _mm(a, b):
    return jnp.dot(a.astype(MXU_DTYPE), b.astype(MXU_DTYPE), preferred_element_type=jnp.float32)


def _mm_nt(a, b):
    return lax.dot_general(a.astype(MXU_DTYPE), b.astype(MXU_DTYPE), (((1,), (1,)), ((), ())),
                           preferred_element_type=jnp.float32)


def _rms_rows(x):
    return x * lax.rsqrt(jnp.mean(x * x, axis=-1, keepdims=True) + EPS)


def _group_mean_sq(z, bd):
    sq = z * z
    hi = sq.astype(jnp.bfloat16)
    lo = (sq - hi.astype(jnp.float32)).astype(jnp.bfloat16)
    return (jnp.dot(hi, bd, preferred_element_type=jnp.float32)
            + jnp.dot(lo, bd, preferred_element_type=jnp.float32))


def _ada_kernel(c_ref, w_ref, b_ref, o_ref):
    o_ref[...] = jnp.dot(c_ref[...], w_ref[...], precision=_HI,
                         preferred_element_type=jnp.float32) + b_ref[...]


def _ada_call(c, w, b):
    bsz, d = c.shape
    n_out = w.shape[1]
    tn = 1024
    return pl.pallas_call(
        _ada_kernel,
        grid=(n_out // tn,),
        in_specs=[pl.BlockSpec((bsz, d), lambda j: (0, 0)),
                  pl.BlockSpec((d, tn), lambda j: (0, j)),
                  pl.BlockSpec((1, tn), lambda j: (0, j))],
        out_specs=pl.BlockSpec((bsz, tn), lambda j: (0, j)),
        out_shape=jax.ShapeDtypeStruct((bsz, n_out), jnp.float32),
        compiler_params=_cparams(("arbitrary",)),
        name="ada",
    )(c, w, b.reshape(1, n_out))


def _inproj_kernel(x_ref, ada_ref, g1_ref, wa_ref, wp_ref, bd_a_ref, bd_b_ref, gcol_ref,
                   q_ref, k_ref, v_ref, iq_ref, ik_ref, iw_ref, zp_ref):
    x = x_ref[...]
    shift1 = ada_ref[0, 0:1, :]
    scale1 = ada_ref[0, 1:2, :]
    h = (_rms_rows(x) * g1_ref[...]) * (1.0 + scale1) + shift1
    hb = h.astype(MXU_DTYPE)
    z = jnp.dot(hb, wa_ref[...], preferred_element_type=jnp.float32)
    zp_ref[...] = jnp.dot(hb, wp_ref[...], preferred_element_type=jnp.float32)

    za = z[:, 0:_IQ0]
    na = za * lax.rsqrt(_group_mean_sq(za, bd_a_ref[...]) + EPS) * gcol_ref[:, 0:_IQ0]
    for hh in range(N_HEADS):
        q_ref[hh] = na[:, hh * HEAD_DIM:(hh + 1) * HEAD_DIM].astype(q_ref.dtype)
    k_ref[...] = na[:, _K0:_K0 + HEAD_DIM].astype(k_ref.dtype)
    v_ref[...] = z[:, _V0:_V0 + HEAD_DIM].astype(v_ref.dtype)
    for hh in range(N_IDX_HEADS):
        iq_ref[hh] = (z[:, _IQ0 + hh * IDX_DIM:_IQ0 + (hh + 1) * IDX_DIM] * (IDX_DIM ** -0.5)).astype(iq_ref.dtype)
    zb = z[:, _IK0:_SLAB_A]
    nb = zb * lax.rsqrt(_group_mean_sq(zb, bd_b_ref[...]) + EPS) * gcol_ref[:, _IK0:_SLAB_A]
    ik_ref[...] = nb[:, 0:IDX_DIM].astype(ik_ref.dtype)
    iw_ref[...] = zb[:, IDX_DIM:IDX_DIM + N_IDX_HEADS] * (N_IDX_HEADS ** -0.5)


def _inproj_call(x2, ada3, g1, wa, wp, bd_a, bd_b, gcol, seq, tm):
    n, d = x2.shape
    tiles_per_seq = seq // tm
    f32 = jnp.float32
    out_shape = (
        jax.ShapeDtypeStruct((N_HEADS, n, HEAD_DIM), MXU_DTYPE),
        jax.ShapeDtypeStruct((n, HEAD_DIM), MXU_DTYPE),
        jax.ShapeDtypeStruct((n, HEAD_DIM), MXU_DTYPE),
        jax.ShapeDtypeStruct((N_IDX_HEADS, n, IDX_DIM), MXU_DTYPE),
        jax.ShapeDtypeStruct((n, IDX_DIM), MXU_DTYPE),
        jax.ShapeDtypeStruct((n, N_IDX_HEADS), f32),
        jax.ShapeDtypeStruct((n, POOL_WIDTH), f32),
    )
    const2 = lambda i: (0, 0)
    return pl.pallas_call(
        _inproj_kernel,
        grid=(n // tm,),
        in_specs=[pl.BlockSpec((tm, d), lambda i: (i, 0)),
                  pl.BlockSpec((1, 6, d), lambda i: (i // tiles_per_seq, 0, 0)),
                  pl.BlockSpec((1, d), const2),
                  pl.BlockSpec(wa.shape, const2),
                  pl.BlockSpec(wp.shape, const2),
                  pl.BlockSpec(bd_a.shape, const2),
                  pl.BlockSpec(bd_b.shape, const2),
                  pl.BlockSpec(gcol.shape, const2)],
        out_specs=(pl.BlockSpec((N_HEADS, tm, HEAD_DIM), lambda i: (0, i, 0)),
                   pl.BlockSpec((tm, HEAD_DIM), lambda i: (i, 0)),
                   pl.BlockSpec((tm, HEAD_DIM), lambda i: (i, 0)),
                   pl.BlockSpec((N_IDX_HEADS, tm, IDX_DIM), lambda i: (0, i, 0)),
                   pl.BlockSpec((tm, IDX_DIM), lambda i: (i, 0)),
                   pl.BlockSpec((tm, N_IDX_HEADS), lambda i: (i, 0)),
                   pl.BlockSpec((tm, POOL_WIDTH), lambda i: (i, 0))),
        out_shape=out_shape,
        compiler_params=_cparams(("parallel",)),
        name="inproj",
    )(x2, ada3, g1, wa, wp, bd_a, bd_b, gcol)


def _sortable_key(score):
    bits = lax.bitcast_convert_type(score + 0.0, jnp.int32)
    return bits ^ (lax.shift_right_arithmetic(bits, 31) & 0x7FFFFFFF)


def _dsa_kernel(q_ref, iq_ref, iw_ref, k_ref, v_ref, ik_ref, ut_ref, o_ref, key_ref, *, n_sel, tq):
    seq = k_ref.shape[0]
    blk = pl.program_id(1)
    ik = ik_ref[...]
    iw = iw_ref[...]
    score = jnp.zeros((tq, seq), jnp.float32)
    for hh in range(N_IDX_HEADS):
        rel = jnp.maximum(_mm_nt(iq_ref[hh], ik), 0.0)
        score = score + rel * iw[:, hh:hh + 1]
    q_chunk = (blk * tq + lax.broadcasted_iota(jnp.int32, (tq, seq), 0)) // CHUNK
    k_chunk = lax.broadcasted_iota(jnp.int32, (tq, seq), 1) // CHUNK
    adm = k_chunk <= q_chunk
    key_ref[...] = jnp.where(adm, _sortable_key(score), _INT_MIN)

    kf = float(n_sel)

    def count_ge(cand):
        return jnp.sum(jnp.where(key_ref[...] >= cand, 1.0, 0.0), axis=1, keepdims=True)

    thr = jnp.where(count_ge(jnp.zeros((tq, 1), jnp.int32)) >= kf, 0, _INT_MIN).astype(jnp.int32)

    def bit_step(i, thr):
        cand = thr | lax.shift_left(jnp.int32(1), 30 - i)
        return jnp.where(count_ge(cand) >= kf, cand, thr)

    thr = lax.fori_loop(0, 31, bit_step, thr)

    key = key_ref[...]
    gt = key > thr
    eq = key == thr
    n_gt = jnp.sum(jnp.where(gt, 1.0, 0.0), axis=1, keepdims=True)
    prefix = jnp.dot(jnp.where(eq, 1.0, 0.0).astype(jnp.bfloat16), ut_ref[...],
                     preferred_element_type=jnp.float32)
    sel = (gt | (eq & (prefix <= kf - n_gt))) & adm

    k = k_ref[...]
    v = v_ref[...]
    outs = []
    for hh in range(N_HEADS):
        logits = jnp.where(sel, _mm_nt(q_ref[hh], k), -jnp.inf)
        m = jnp.max(logits, axis=1, keepdims=True)
        p = jnp.exp(logits - m)
        den = jnp.sum(p, axis=1, keepdims=True)
        outs.append(_mm(p, v) / den)
    o_ref[...] = jnp.concatenate(outs, axis=1).astype(o_ref.dtype)


def _dsa_call(q, iq, iw, k, v, ik, ut, bsz, seq, tq):
    n = k.shape[0]
    n_sel = min(TOPK_MAX, seq // 4)
    nq = seq // tq
    kern = functools.partial(_dsa_kernel, n_sel=n_sel, tq=tq)
    per_batch = lambda b, j: (b, 0)
    return pl.pallas_call(
        kern,
        grid=(bsz, nq),
        in_specs=[pl.BlockSpec((N_HEADS, tq, HEAD_DIM), lambda b, j: (0, b * nq + j, 0)),
                  pl.BlockSpec((N_IDX_HEADS, tq, IDX_DIM), lambda b, j: (0, b * nq + j, 0)),
                  pl.BlockSpec((tq, N_IDX_HEADS), lambda b, j: (b * nq + j, 0)),
                  pl.BlockSpec((seq, HEAD_DIM), per_batch),
                  pl.BlockSpec((seq, HEAD_DIM), per_batch),
                  pl.BlockSpec((seq, IDX_DIM), per_batch),
                  pl.BlockSpec((seq, seq), lambda b, j: (0, 0))],
        out_specs=pl.BlockSpec((tq, ATTN_WIDTH), lambda b, j: (b * nq + j, 0)),
        out_shape=jax.ShapeDtypeStruct((n, ATTN_WIDTH), MXU_DTYPE),
        scratch_shapes=[pltpu.VMEM((tq, seq), jnp.int32)],
        compiler_params=_cparams(("parallel", "parallel")),
        name="dsa",
    )(q, iq, iw, k, v, ik, ut)


def _mix_kernel(x_ref, ada_ref, g1_ref, g2_ref, attn_ref, zp_ref, halo_ref, wg_ref, wgrp_ref, spool_ref,
                wua_ref, wup_ref, wo_ref, wqt_ref, x1_ref, h2t_ref, pqt_ref, *, tiles_per_seq, tm):
    i = pl.program_id(0)
    x = x_ref[...]
    shift1 = ada_ref[0, 0:1, :]
    scale1 = ada_ref[0, 1:2, :]
    gate1 = ada_ref[0, 2:3, :]
    shift2 = ada_ref[0, 3:4, :]
    scale2 = ada_ref[0, 4:5, :]
    h = (_rms_rows(x) * g1_ref[...]) * (1.0 + scale1) + shift1
    zg = jnp.dot(h.astype(MXU_DTYPE), wg_ref[...], preferred_element_type=jnp.float32)
    gates = jax.nn.sigmoid(zg)

    first_in_seq = (i % tiles_per_seq) == 0
    halo = jnp.where(first_in_seq, 0.0, halo_ref[...])
    ext = jnp.concatenate([halo, zp_ref[...]], axis=0)
    t_seq = ((i % tiles_per_seq) * tm + lax.broadcasted_iota(jnp.int32, (tm, 1), 0)).astype(jnp.float32)
    run = ext
    pooled = []
    for g, w in enumerate(POOL_WINDOWS):
        run = run + pltpu.roll(run, w // 2, axis=0)
        cnt = jnp.minimum(t_seq + 1.0, float(w))
        lo, hi = g * POOL_GROUP, (g + 1) * POOL_GROUP
        mean = run[POOL_HALO:, lo:hi] / cnt
        mixed = (mean - ext[POOL_HALO:, lo:hi])
        pooled.append(_mm(mixed, wgrp_ref[g]) * spool_ref[g:g + 1, :])
    pool_out = jnp.concatenate(pooled, axis=1)

    y_attn = jnp.dot(attn_ref[...], wua_ref[...], preferred_element_type=jnp.float32)
    y_pool = _mm(pool_out, wup_ref[...])
    merged = gates[:, 0:D_MODEL] * y_attn + gates[:, D_MODEL:] * y_pool
    x1 = x + gate1 * _mm(merged, wo_ref[...])
    x1_ref[...] = x1

    h2 = (_rms_rows(x1) * g2_ref[...]) * (1.0 + scale2) + shift2
    h2t = h2.T.astype(MXU_DTYPE)
    h2t_ref[...] = h2t
    pqt_ref[...] = jnp.dot(wqt_ref[...], h2t, preferred_element_type=jnp.float32)


def _mix_call(x2, ada3, g1, g2, attn, zp, wg, wgrp, spool, wua, wup, wo, wqt, seq, tm):
    n, d = x2.shape
    tiles_per_seq = seq // tm
    halo_per_tile = tm // POOL_HALO
    kern = functools.partial(_mix_kernel, tiles_per_seq=tiles_per_seq, tm=tm)
    const2 = lambda i: (0, 0)
    const3 = lambda i: (0, 0, 0)
    return pl.pallas_call(
        kern,
        grid=(n // tm,),
        in_specs=[pl.BlockSpec((tm, d), lambda i: (i, 0)),
                  pl.BlockSpec((1, 6, d), lambda i: (i // tiles_per_seq, 0, 0)),
                  pl.BlockSpec((1, d), const2),
                  pl.BlockSpec((1, d), const2),
                  pl.BlockSpec((tm, ATTN_WIDTH), lambda i: (i, 0)),
                  pl.BlockSpec((tm, POOL_WIDTH), lambda i: (i, 0)),
                  pl.BlockSpec((POOL_HALO, POOL_WIDTH), lambda i: (jnp.maximum(i * halo_per_tile - 1, 0), 0)),
                  pl.BlockSpec(wg.shape, const2),
                  pl.BlockSpec(wgrp.shape, const3),
                  pl.BlockSpec(spool.shape, const2),
                  pl.BlockSpec(wua.shape, const2),
                  pl.BlockSpec(wup.shape, const2),
                  pl.BlockSpec(wo.shape, const2),
                  pl.BlockSpec(wqt.shape, const2)],
        out_specs=(pl.BlockSpec((tm, d), lambda i: (i, 0)),
                   pl.BlockSpec((d, tm), lambda i: (0, i)),
                   pl.BlockSpec((d, tm), lambda i: (0, i))),
        out_shape=(jax.ShapeDtypeStruct((n, d), jnp.float32),
                   jax.ShapeDtypeStruct((d, n), MXU_DTYPE),
                   jax.ShapeDtypeStruct((d, n), jnp.float32)),
        compiler_params=_cparams(("parallel",)),
        name="mix",
    )(x2, ada3, g1, g2, attn, zp, zp, wg, wgrp, spool, wua, wup, wo, wqt)


def _extract_top(s, n_rounds):
    r, t = s.shape
    rows = lax.broadcasted_iota(jnp.int32, (r, t), 0).astype(jnp.float32)
    rank = jnp.full((r, t), float(n_rounds), jnp.float32)
    vals, firsts = [], []
    for rnd in range(n_rounds):
        m = jnp.max(s, axis=0, keepdims=True)
        first = jnp.min(jnp.where(s == m, rows, float(r)), axis=0, keepdims=True)
        hit = rows == first
        rank = jnp.where(hit, float(rnd), rank)
        s = jnp.where(hit, -jnp.inf, s)
        vals.append(m)
        firsts.append(first)
    return vals, firsts, rank


def _peer_select_kernel(pqt_ref, sub_ref, c_ref, p1_ref, r2_ref, p2_ref):
    te = pqt_ref.shape[1]
    rows16 = lax.broadcasted_iota(jnp.int32, (PEER_TOPK, te), 0).astype(jnp.float32)

    def head_step(hh, carry):
        s, tv, rk = [], [], []
        for half in range(2):
            qrows = pqt_ref[pl.ds(pl.multiple_of((hh * 2 + half) * PEER_HALF, PEER_HALF), PEER_HALF), :]
            sc = jnp.dot(sub_ref[hh, half], qrows, precision=_HI, preferred_element_type=jnp.float32)
            vals, _, rank = _extract_top(sc, PEER_TOPK)
            sorted_vals = jnp.zeros((PEER_TOPK, te), jnp.float32)
            for r in range(PEER_TOPK):
                sorted_vals = jnp.where(rows16 == float(r), vals[r], sorted_vals)
            s.append(sc)
            tv.append(sorted_vals)
            rk.append(rank)
        cand = (tv[0][:, None, :] + tv[1][None, :, :]).reshape(PEER_TOPK * PEER_TOPK, te)
        best, firsts, _ = _extract_top(cand, PEER_TOPK)
        counts = jnp.zeros((PEER_TOPK, te), jnp.float32)
        zsum = jnp.zeros((1, te), jnp.float32)
        for j in range(PEER_TOPK):
            counts = counts + jnp.where(rows16 == jnp.floor(firsts[j] * (1.0 / PEER_TOPK)), 1.0, 0.0)
            zsum = zsum + jnp.exp(best[j] - best[0])
        c_dense = jnp.zeros((N_KEYS, te), jnp.float32)
        for r in range(PEER_TOPK):
            c_dense = jnp.where(rk[0] == float(r), counts[r:r + 1, :], c_dense)
        c_ref[hh] = c_dense
        p1_ref[hh] = jnp.where(rk[0] < float(PEER_TOPK), jnp.exp(s[0] - tv[0][0:1, :]), 0.0) / zsum
        r2_ref[hh] = rk[1]
        p2_ref[hh] = jnp.where(rk[1] < float(PEER_TOPK), jnp.exp(s[1] - tv[1][0:1, :]), 0.0)
        return carry

    lax.fori_loop(0, PEER_HEADS, head_step, 0)


def _peer_select_call(pqt, sub, te):
    d, n = pqt.shape
    shp = jax.ShapeDtypeStruct((PEER_HEADS, N_KEYS, n), jnp.float32)
    ospec = pl.BlockSpec((PEER_HEADS, N_KEYS, te), lambda i: (0, 0, i))
    return pl.pallas_call(
        _peer_select_kernel,
        grid=(n // te,),
        in_specs=[pl.BlockSpec((d, te), lambda i: (0, i)),
                  pl.BlockSpec(sub.shape, lambda i: (0, 0, 0, 0))],
        out_specs=(ospec, ospec, ospec, ospec),
        out_shape=(shp, shp, shp, shp),
        compiler_params=_cparams(("parallel",)),
        name="peer_select",
    )(pqt, sub)


def _peer_dense_kernel(h2t_ref, u_ref, vt_ref, c_ref, p1_ref, r2_ref, p2_ref, x1_ref, ada_ref,
                       o_ref, acc_ref, g_ref, *, keys_per_step):
    j = pl.program_id(1)

    @pl.when(j == 0)
    def _():
        acc_ref[...] = jnp.zeros_like(acc_ref)

    a = jnp.dot(u_ref[...], h2t_ref[...], preferred_element_type=jnp.float32)
    act = 0.5 * a * (1.0 + lax.erf(a * (2.0 ** -0.5)))
    for kk in range(keys_per_step):
        i1 = j * keys_per_step + kk
        w = jnp.zeros((N_KEYS, a.shape[1]), jnp.float32)
        for hh in range(PEER_HEADS):
            cb = c_ref[hh, pl.ds(i1, 1), :]
            pb = p1_ref[hh, pl.ds(i1, 1), :]
            w = w + jnp.where(r2_ref[hh] < cb, p2_ref[hh], 0.0) * pb
        g_ref[kk * N_KEYS:(kk + 1) * N_KEYS, :] = (w * act[kk * N_KEYS:(kk + 1) * N_KEYS, :]).astype(g_ref.dtype)
    acc_ref[...] += jnp.dot(vt_ref[...], g_ref[...], preferred_element_type=jnp.float32)

    @pl.when(j == pl.num_programs(1) - 1)
    def _():
        gate2 = ada_ref[0, 5:6, :]
        o_ref[...] = x1_ref[...] + gate2 * acc_ref[...].T


def _peer_dense_call(h2t, u, vt, c_d, p1_d, r2_d, p2_d, x1, ada3, seq, tt, ec):
    d, n = h2t.shape
    n_exp = u.shape[0]
    tiles_per_seq = seq // tt
    keys_per_step = ec // N_KEYS
    kern = functools.partial(_peer_dense_kernel, keys_per_step=keys_per_step)
    sel_spec = pl.BlockSpec((PEER_HEADS, N_KEYS, tt), lambda i, j: (0, 0, i))
    return pl.pallas_call(
        kern,
        grid=(n // tt, n_exp // ec),
        in_specs=[pl.BlockSpec((d, tt), lambda i, j: (0, i)),
                  pl.BlockSpec((ec, d), lambda i, j: (j, 0)),
                  pl.BlockSpec((d, ec), lambda i, j: (0, j)),
                  sel_spec, sel_spec, sel_spec, sel_spec,
                  pl.BlockSpec((tt, d), lambda i, j: (i, 0)),
                  pl.BlockSpec((1, 6, d), lambda i, j: (i // tiles_per_seq, 0, 0))],
        out_specs=pl.BlockSpec((tt, d), lambda i, j: (i, 0)),
        out_shape=jax.ShapeDtypeStruct((n, d), jnp.float32),
        scratch_shapes=[pltpu.VMEM((d, tt), jnp.float32), pltpu.VMEM((ec, tt), MXU_DTYPE)],
        compiler_params=_cparams(("parallel", "arbitrary")),
        name="peer_dense",
    )(h2t, u, vt, c_d, p1_d, r2_d, p2_d, x1, ada3)


def _block_diag_mean(width, group):
    idx = np.arange(width) // group
    return jnp.asarray((idx[:, None] == idx[None, :]).astype(np.float32) / group, dtype=jnp.bfloat16)


def _layer(x2, c, w_ada, b_ada, g_norm1, w_in, g_q, g_k, g_ik, w_pool_grp, s_pool, w_up_attn, w_up_pool,
           w_out, g_norm2, w_peer_q, peer_subkeys, peer_u, peer_v, bsz, seq):
    f32 = jnp.float32
    d = D_MODEL
    tm_b = min(512, seq)
    tq = min(128, seq)
    tm_d = min(256, seq)
    te = 128
    tt = min(512, seq)
    ec = 512

    ada3 = _ada_call(c, w_ada, b_ada).reshape(bsz, 6, d)

    wa = jnp.pad(w_in[:, :_ZP0], ((0, 0), (0, _SLAB_A - _ZP0))).astype(MXU_DTYPE)
    wp = w_in[:, _ZP0:_ZG0].astype(MXU_DTYPE)
    wg = w_in[:, _ZG0:].astype(MXU_DTYPE)
    gcol = jnp.concatenate([jnp.tile(g_q, N_HEADS) * (HEAD_DIM ** -0.5), g_k, jnp.ones((_IK0 - _V0,), f32),
                            g_ik, jnp.ones((_SLAB_A - _IK0 - IDX_DIM,), f32)]).reshape(1, _SLAB_A)
    bd_a = _block_diag_mean(_IQ0, HEAD_DIM)
    bd_b = _block_diag_mean(_SLAB_A - _IK0, IDX_DIM)

    q, k, v, iq, ik, iw, zp = _inproj_call(x2, ada3, g_norm1.reshape(1, d), wa, wp, bd_a, bd_b, gcol, seq, tm_b)

    ut = jnp.asarray(np.triu(np.ones((seq, seq), np.float32)), dtype=jnp.bfloat16)
    attn = _dsa_call(q, iq, iw, k, v, ik, ut, bsz, seq, tq)

    x1, h2t, pqt = _mix_call(x2, ada3, g_norm1.reshape(1, d), g_norm2.reshape(1, d), attn, zp, wg,
                             w_pool_grp.astype(MXU_DTYPE), s_pool, w_up_attn.astype(MXU_DTYPE),
                             w_up_pool.astype(MXU_DTYPE), w_out.astype(MXU_DTYPE),
                             w_peer_q.T.astype(MXU_DTYPE), seq, tm_d)

    c_d, p1_d, r2_d, p2_d = _peer_select_call(pqt, peer_subkeys, te)

    return _peer_dense_call(h2t, peer_u.astype(MXU_DTYPE), peer_v.T.astype(MXU_DTYPE),
                            c_d, p1_d, r2_d, p2_d, x1, ada3, seq, tt, ec)


def kernel(x, c, w_ada, b_ada, g_norm1, w_in, g_q, g_k, g_ik, w_pool_grp, s_pool, w_up_attn, w_up_pool, w_out,
           g_norm2, w_peer_q, peer_subkeys, peer_u, peer_v):
    bsz, seq, d = x.shape
    x2 = x.reshape(bsz * seq, d)
    for layer in range(w_ada.shape[0]):
        x2 = _layer(x2, c, w_ada[layer], b_ada[layer], g_norm1[layer], w_in[layer], g_q[layer], g_k[layer],
                    g_ik[layer], w_pool_grp[layer], s_pool[layer], w_up_attn[layer], w_up_pool[layer],
                    w_out[layer], g_norm2[layer], w_peer_q[layer], peer_subkeys[layer], peer_u[layer],
                    peer_v[layer], bsz, seq)
    return x2.reshape(bsz, seq, d)
```

```python
import functools

import jax
import jax.numpy as jnp
import numpy as np
from jax import lax
from jax.experimental import pallas as pl
from jax.experimental.pallas import tpu as pltpu

D_MODEL = 1024
CHUNK = 64
N_HEADS = 8
HEAD_DIM = 64
ATTN_WIDTH = N_HEADS * HEAD_DIM
N_IDX_HEADS = 4
IDX_DIM = 64
TOPK_MAX = 256
DSA_SUB = 128
DSA_GROUP = 256
POOL_WINDOWS = (2, 4, 8, 16)
N_POOL_GROUPS = 4
POOL_WIDTH = 512
POOL_GROUP = POOL_WIDTH // N_POOL_GROUPS
PEER_HEADS = 8
PEER_HALF = 64
N_KEYS = 128
PEER_TOPK = 16
PEER_SUB = 256
EPS = 1e-6
POOL_HALO = 16

_Q0, _K0, _V0, _IQ0, _IK0, _IW0 = 0, 512, 576, 640, 896, 960
_SLAB_A = 1024
_ZP0 = 964
_ZG0 = 1476

MXU_DTYPE = jnp.bfloat16
VMEM_LIMIT = 56 * 1024 * 1024

_INT_MIN = -(2 ** 31)
_HI = lax.Precision.HIGHEST


def _cparams(sem):
    return pltpu.CompilerParams(dimension_semantics=sem, vmem_limit_bytes=VMEM_LIMIT)


def _mm(a, b):
    return jnp.dot(a.astype(MXU_DTYPE), b.astype(MXU_DTYPE), preferred_element_type=jnp.float32)


def _mm_nt(a, b):
    return lax.dot_general(a.astype(MXU_DTYPE), b.astype(MXU_DTYPE), (((1,), (1,)), ((), ())),
                           preferred_element_type=jnp.float32)


def _rms_rows(x):
    return x * lax.rsqrt(jnp.mean(x * x, axis=-1, keepdims=True) + EPS)


def _group_mean_sq(z, bd):
    sq = z * z
    hi = sq.astype(jnp.bfloat16)
    lo = (sq - hi.astype(jnp.float32)).astype(jnp.bfloat16)
    return (jnp.dot(hi, bd, preferred_element_type=jnp.float32)
            + jnp.dot(lo, bd, preferred_element_type=jnp.float32))


def _ada_kernel(c_ref, w_ref, b_ref, o_ref):
    o_ref[...] = jnp.dot(c_ref[...], w_ref[...], precision=_HI,
                         preferred_element_type=jnp.float32) + b_ref[...]


def _ada_call(c, w, b):
    bsz, d = c.shape
    n_out = w.shape[1]
    tn = 1024
    return pl.pallas_call(
        _ada_kernel,
        grid=(n_out // tn,),
        in_specs=[pl.BlockSpec((bsz, d), lambda j: (0, 0)),
                  pl.BlockSpec((d, tn), lambda j: (0, j)),
                  pl.BlockSpec((1, tn), lambda j: (0, j))],
        out_specs=pl.BlockSpec((bsz, tn), lambda j: (0, j)),
        out_shape=jax.ShapeDtypeStruct((bsz, n_out), jnp.float32),
        compiler_params=_cparams(("arbitrary",)),
        name="ada",
    )(c, w, b.reshape(1, n_out))


def _inproj_kernel(x_ref, ada_ref, g1_ref, wa_ref, wp_ref, bd_a_ref, bd_b_ref, gcol_ref,
                   q_ref, k_ref, v_ref, iq_ref, ik_ref, iw_ref, zp_ref):
    x = x_ref[...]
    shift1 = ada_ref[0, 0:1, :]
    scale1 = ada_ref[0, 1:2, :]
    h = (_rms_rows(x) * g1_ref[...]) * (1.0 + scale1) + shift1
    hb = h.astype(MXU_DTYPE)
    z = jnp.dot(hb, wa_ref[...], preferred_element_type=jnp.float32)
    zp_ref[...] = jnp.dot(hb, wp_ref[...], preferred_element_type=jnp.float32)

    za = z[:, 0:_IQ0]
    na = za * lax.rsqrt(_group_mean_sq(za, bd_a_ref[...]) + EPS) * gcol_ref[:, 0:_IQ0]
    for hh in range(N_HEADS):
        q_ref[hh] = na[:, hh * HEAD_DIM:(hh + 1) * HEAD_DIM].astype(q_ref.dtype)
    k_ref[...] = na[:, _K0:_K0 + HEAD_DIM].astype(k_ref.dtype)
    v_ref[...] = z[:, _V0:_V0 + HEAD_DIM].astype(v_ref.dtype)
    for hh in range(N_IDX_HEADS):
        iq_ref[hh] = (z[:, _IQ0 + hh * IDX_DIM:_IQ0 + (hh + 1) * IDX_DIM] * (IDX_DIM ** -0.5)).astype(iq_ref.dtype)
    zb = z[:, _IK0:_SLAB_A]
    nb = zb * lax.rsqrt(_group_mean_sq(zb, bd_b_ref[...]) + EPS) * gcol_ref[:, _IK0:_SLAB_A]
    ik_ref[...] = nb[:, 0:IDX_DIM].astype(ik_ref.dtype)
    iw_ref[...] = zb[:, IDX_DIM:IDX_DIM + N_IDX_HEADS] * (N_IDX_HEADS ** -0.5)


def _inproj_call(x2, ada3, g1, wa, wp, bd_a, bd_b, gcol, seq, tm):
    n, d = x2.shape
    tiles_per_seq = seq // tm
    f32 = jnp.float32
    out_shape = (
        jax.ShapeDtypeStruct((N_HEADS, n, HEAD_DIM), MXU_DTYPE),
        jax.ShapeDtypeStruct((n, HEAD_DIM), MXU_DTYPE),
        jax.ShapeDtypeStruct((n, HEAD_DIM), MXU_DTYPE),
        jax.ShapeDtypeStruct((N_IDX_HEADS, n, IDX_DIM), MXU_DTYPE),
        jax.ShapeDtypeStruct((n, IDX_DIM), MXU_DTYPE),
        jax.ShapeDtypeStruct((n, N_IDX_HEADS), f32),
        jax.ShapeDtypeStruct((n, POOL_WIDTH), f32),
    )
    const2 = lambda i: (0, 0)
    return pl.pallas_call(
        _inproj_kernel,
        grid=(n // tm,),
        in_specs=[pl.BlockSpec((tm, d), lambda i: (i, 0)),
                  pl.BlockSpec((1, 6, d), lambda i: (i // tiles_per_seq, 0, 0)),
                  pl.BlockSpec((1, d), const2),
                  pl.BlockSpec(wa.shape, const2),
                  pl.BlockSpec(wp.shape, const2),
                  pl.BlockSpec(bd_a.shape, const2),
                  pl.BlockSpec(bd_b.shape, const2),
                  pl.BlockSpec(gcol.shape, const2)],
        out_specs=(pl.BlockSpec((N_HEADS, tm, HEAD_DIM), lambda i: (0, i, 0)),
                   pl.BlockSpec((tm, HEAD_DIM), lambda i: (i, 0)),
                   pl.BlockSpec((tm, HEAD_DIM), lambda i: (i, 0)),
                   pl.BlockSpec((N_IDX_HEADS, tm, IDX_DIM), lambda i: (0, i, 0)),
                   pl.BlockSpec((tm, IDX_DIM), lambda i: (i, 0)),
                   pl.BlockSpec((tm, N_IDX_HEADS), lambda i: (i, 0)),
                   pl.BlockSpec((tm, POOL_WIDTH), lambda i: (i, 0))),
        out_shape=out_shape,
        compiler_params=_cparams(("parallel",)),
        name="inproj",
    )(x2, ada3, g1, wa, wp, bd_a, bd_b, gcol)


def _sortable_key(score):
    bits = lax.bitcast_convert_type(score + 0.0, jnp.int32)
    return bits ^ (lax.shift_right_arithmetic(bits, 31) & 0x7FFFFFFF)


def _dsa_block(q_ref, iq_ref, iw_ref, k_ref, v_ref, ik_ref, ut_ref, ones_ref, o_ref, key_ref, bias_ref,
               *, n_sel, tq, blk):
    base = blk * tq
    width = base + tq
    kf = float(n_sel)

    ik = ik_ref[0:width, :]
    iw = iw_ref[...]
    score = None
    for hh in range(N_IDX_HEADS):
        term = jnp.maximum(_mm_nt(iq_ref[hh], ik), 0.0) * iw[:, hh:hh + 1]
        score = term if score is None else score + term
    q_chunk = (base + lax.broadcasted_iota(jnp.int32, (tq, width), 0)) // CHUNK
    k_chunk = lax.broadcasted_iota(jnp.int32, (tq, width), 1) // CHUNK
    key_ref[:, 0:width] = jnp.where(k_chunk <= q_chunk, _sortable_key(score), _INT_MIN)

    def count_ge(cand):
        return jnp.sum(jnp.where(key_ref[:, 0:width] >= cand, 1.0, 0.0), axis=1, keepdims=True)

    thr = jnp.where(count_ge(jnp.zeros((tq, 1), jnp.int32)) >= kf, 0, _INT_MIN).astype(jnp.int32)

    def bit_step(i, thr):
        cand = thr | lax.shift_left(jnp.int32(1), 30 - i)
        return jnp.where(count_ge(cand) >= kf, cand, thr)

    thr = lax.fori_loop(0, 31, bit_step, thr)

    room = kf - jnp.sum(jnp.where(key_ref[:, 0:width] > thr, 1.0, 0.0), axis=1, keepdims=True)
    seen = jnp.zeros((tq, DSA_GROUP), jnp.float32)
    for g in range(width // DSA_GROUP):
        cols = slice(g * DSA_GROUP, (g + 1) * DSA_GROUP)
        key = key_ref[:, cols]
        eq = key == thr
        eqb = jnp.where(eq, 1.0, 0.0).astype(jnp.bfloat16)
        prefix = seen + jnp.dot(eqb, ut_ref[...], preferred_element_type=jnp.float32)
        seen = seen + jnp.dot(eqb, ones_ref[...], preferred_element_type=jnp.float32)
        sel = ((key > thr) | (eq & (prefix <= room))) & (key != _INT_MIN)
        bias_ref[:, cols] = jnp.where(sel, 0.0, -jnp.inf)

    for sub in range(tq // DSA_SUB):
        rows = slice(sub * DSA_SUB, (sub + 1) * DSA_SUB)
        ws = base + (sub + 1) * DSA_SUB
        qs = q_ref[:, rows, :].reshape(N_HEADS * DSA_SUB, HEAD_DIM)
        logits = _mm_nt(qs, k_ref[0:ws, :]).reshape(N_HEADS, DSA_SUB, ws) + bias_ref[rows, 0:ws][None]
        m = jnp.max(logits, axis=2, keepdims=True)
        p = jnp.exp(logits - m)
        den = jnp.sum(p, axis=2, keepdims=True)
        pv = _mm(p.reshape(N_HEADS * DSA_SUB, ws), v_ref[0:ws, :]).reshape(N_HEADS, DSA_SUB, HEAD_DIM) / den
        o_ref[rows, :] = jnp.concatenate([pv[hh] for hh in range(N_HEADS)], axis=1).astype(o_ref.dtype)


def _dsa_kernel(*refs, n_sel, tq):
    seq = refs[3].shape[0]
    for blk in range(seq // tq):
        pl.when(pl.program_id(1) == blk)(functools.partial(_dsa_block, *refs, n_sel=n_sel, tq=tq, blk=blk))


def _dsa_call(q, iq, iw, k, v, ik, bsz, seq, tq):
    n = k.shape[0]
    n_sel = min(TOPK_MAX, seq // 4)
    nq = seq // tq
    ut = jnp.asarray(np.triu(np.ones((DSA_GROUP, DSA_GROUP), np.float32)), dtype=jnp.bfloat16)
    ones = jnp.ones((DSA_GROUP, DSA_GROUP), jnp.bfloat16)
    kern = functools.partial(_dsa_kernel, n_sel=n_sel, tq=tq)
    per_batch = lambda b, j: (b, 0)
    const2 = lambda b, j: (0, 0)
    return pl.pallas_call(
        kern,
        grid=(bsz, nq),
        in_specs=[pl.BlockSpec((N_HEADS, tq, HEAD_DIM), lambda b, j: (0, b * nq + j, 0)),
                  pl.BlockSpec((N_IDX_HEADS, tq, IDX_DIM), lambda b, j: (0, b * nq + j, 0)),
                  pl.BlockSpec((tq, N_IDX_HEADS), lambda b, j: (b * nq + j, 0)),
                  pl.BlockSpec((seq, HEAD_DIM), per_batch),
                  pl.BlockSpec((seq, HEAD_DIM), per_batch),
                  pl.BlockSpec((seq, IDX_DIM), per_batch),
                  pl.BlockSpec(ut.shape, const2),
                  pl.BlockSpec(ones.shape, const2)],
        out_specs=pl.BlockSpec((tq, ATTN_WIDTH), lambda b, j: (b * nq + j, 0)),
        out_shape=jax.ShapeDtypeStruct((n, ATTN_WIDTH), MXU_DTYPE),
        scratch_shapes=[pltpu.VMEM((tq, seq), jnp.int32), pltpu.VMEM((tq, seq), jnp.float32)],
        compiler_params=_cparams(("parallel", "parallel")),
        name="dsa",
    )(q, iq, iw, k, v, ik, ut, ones)


def _mix_kernel(x_ref, ada_ref, g1_ref, g2_ref, attn_ref, zp_ref, halo_ref, wg_ref, wgrp_ref, spool_ref,
                wua_ref, wup_ref, wo_ref, wqt_ref, x1_ref, h2t_ref, pqt_ref, *, tiles_per_seq, tm):
    i = pl.program_id(0)
    x = x_ref[...]
    shift1 = ada_ref[0, 0:1, :]
    scale1 = ada_ref[0, 1:2, :]
    gate1 = ada_ref[0, 2:3, :]
    shift2 = ada_ref[0, 3:4, :]
    scale2 = ada_ref[0, 4:5, :]
    h = (_rms_rows(x) * g1_ref[...]) * (1.0 + scale1) + shift1
    zg = jnp.dot(h.astype(MXU_DTYPE), wg_ref[...], preferred_element_type=jnp.float32)
    gates = jax.nn.sigmoid(zg)

    first_in_seq = (i % tiles_per_seq) == 0
    halo = jnp.where(first_in_seq, 0.0, halo_ref[...])
    ext = jnp.concatenate([halo, zp_ref[...]], axis=0)
    t_seq = ((i % tiles_per_seq) * tm + lax.broadcasted_iota(jnp.int32, (tm, 1), 0)).astype(jnp.float32)
    run = ext
    pooled = []
    for g, w in enumerate(POOL_WINDOWS):
        run = run + pltpu.roll(run, w // 2, axis=0)
        cnt = jnp.minimum(t_seq + 1.0, float(w))
        lo, hi = g * POOL_GROUP, (g + 1) * POOL_GROUP
        mean = run[POOL_HALO:, lo:hi] / cnt
        mixed = (mean - ext[POOL_HALO:, lo:hi])
        pooled.append(_mm(mixed, wgrp_ref[g]) * spool_ref[g:g + 1, :])
    pool_out = jnp.concatenate(pooled, axis=1)

    y_attn = jnp.dot(attn_ref[...], wua_ref[...], preferred_element_type=jnp.float32)
    y_pool = _mm(pool_out, wup_ref[...])
    merged = gates[:, 0:D_MODEL] * y_attn + gates[:, D_MODEL:] * y_pool
    x1 = x + gate1 * _mm(merged, wo_ref[...])
    x1_ref[...] = x1

    h2 = (_rms_rows(x1) * g2_ref[...]) * (1.0 + scale2) + shift2
    h2t = h2.T.astype(MXU_DTYPE)
    h2t_ref[...] = h2t
    pqt_ref[...] = jnp.dot(wqt_ref[...], h2t, preferred_element_type=jnp.float32)


def _mix_call(x2, ada3, g1, g2, attn, zp, wg, wgrp, spool, wua, wup, wo, wqt, seq, tm):
    n, d = x2.shape
    tiles_per_seq = seq // tm
    halo_per_tile = tm // POOL_HALO
    kern = functools.partial(_mix_kernel, tiles_per_seq=tiles_per_seq, tm=tm)
    const2 = lambda i: (0, 0)
    const3 = lambda i: (0, 0, 0)
    return pl.pallas_call(
        kern,
        grid=(n // tm,),
        in_specs=[pl.BlockSpec((tm, d), lambda i: (i, 0)),
                  pl.BlockSpec((1, 6, d), lambda i: (i // tiles_per_seq, 0, 0)),
                  pl.BlockSpec((1, d), const2),
                  pl.BlockSpec((1, d), const2),
                  pl.BlockSpec((tm, ATTN_WIDTH), lambda i: (i, 0)),
                  pl.BlockSpec((tm, POOL_WIDTH), lambda i: (i, 0)),
                  pl.BlockSpec((POOL_HALO, POOL_WIDTH), lambda i: (jnp.maximum(i * halo_per_tile - 1, 0), 0)),
                  pl.BlockSpec(wg.shape, const2),
                  pl.BlockSpec(wgrp.shape, const3),
                  pl.BlockSpec(spool.shape, const2),
                  pl.BlockSpec(wua.shape, const2),
                  pl.BlockSpec(wup.shape, const2),
                  pl.BlockSpec(wo.shape, const2),
                  pl.BlockSpec(wqt.shape, const2)],
        out_specs=(pl.BlockSpec((tm, d), lambda i: (i, 0)),
                   pl.BlockSpec((d, tm), lambda i: (0, i)),
                   pl.BlockSpec((d, tm), lambda i: (0, i))),
        out_shape=(jax.ShapeDtypeStruct((n, d), jnp.float32),
                   jax.ShapeDtypeStruct((d, n), MXU_DTYPE),
                   jax.ShapeDtypeStruct((d, n), jnp.float32)),
        compiler_params=_cparams(("parallel",)),
        name="mix",
    )(x2, ada3, g1, g2, attn, zp, zp, wg, wgrp, spool, wua, wup, wo, wqt)


_STAIR = [(r1, r2) for r1 in range(PEER_TOPK) for r2 in range(PEER_TOPK // (r1 + 1))]
_STAIR_ROWS = -(-len(_STAIR) // 8) * 8


def _stair_maps(te):
    r1 = np.full((_STAIR_ROWS, 1), -1.0, np.float32)
    r2 = np.full((_STAIR_ROWS, 1), -1.0, np.float32)
    for row, (a, b) in enumerate(_STAIR):
        r1[row, 0], r2[row, 0] = a, b
    return jnp.asarray(np.tile(r1, (1, te))), jnp.asarray(np.tile(r2, (1, te)))


def _extract_top(s, n_rounds, tie_exact):
    r, t = s.shape
    rows = lax.broadcasted_iota(jnp.int32, (r, t), 0).astype(jnp.float32)
    rank = jnp.full((r, t), float(n_rounds), jnp.float32)
    vals = []
    for rnd in range(n_rounds):
        m = jnp.max(s, axis=0, keepdims=True)
        hit = s == m
        if tie_exact:
            first = jnp.min(jnp.where(hit, rows, float(r)), axis=0, keepdims=True)
            hit = rows == first
        rank = jnp.where(hit, float(rnd), rank)
        s = jnp.where(hit, -jnp.inf, s)
        vals.append(m)
    return vals, rank


def _count_rows(mask):
    return jnp.sum(jnp.where(mask, 1.0, 0.0), axis=0, keepdims=True)


def _head_select(sc0, sc1, r1map, r2map, tie_exact):
    kf = float(PEER_TOPK)
    vals0, rk0 = _extract_top(sc0, PEER_TOPK, tie_exact)
    vals1, rk1 = _extract_top(sc1, PEER_TOPK, tie_exact)
    a1 = jnp.full(r1map.shape, -jnp.inf, jnp.float32)
    a2 = jnp.zeros(r2map.shape, jnp.float32)
    for r in range(PEER_TOPK):
        a1 = jnp.where(r1map == float(r), vals0[r], a1)
        a2 = jnp.where(r2map == float(r), vals1[r], a2)
    best, rkc = _extract_top(a1 + a2, PEER_TOPK, tie_exact)
    picked = rkc < kf
    c_dense = jnp.zeros(sc0.shape, jnp.float32)
    for r in range(PEER_TOPK):
        c_dense = jnp.where(rk0 == float(r), _count_rows(picked & (r1map == float(r))), c_dense)
    zsum = jnp.ones_like(best[0])
    for j in range(1, PEER_TOPK):
        zsum = zsum + jnp.exp(best[j] - best[0])
    p1 = jnp.where(rk0 < kf, jnp.exp(sc0 - vals0[0]), 0.0) / zsum
    p2 = jnp.where(rk1 < kf, jnp.exp(sc1 - vals1[0]), 0.0)
    bad = (_count_rows(rk0 < kf) != kf) | (_count_rows(rk1 < kf) != kf) | (_count_rows(picked) != kf)
    return c_dense, p1, rk1, p2, jnp.sum(jnp.where(bad, 1.0, 0.0))


def _peer_select_kernel(pqt_ref, sub_ref, r1map_ref, r2map_ref, c_ref, p1_ref, r2_ref, p2_ref):
    def head_step(hh, carry):
        sc = []
        for half in range(2):
            qrows = pqt_ref[pl.ds(pl.multiple_of((hh * 2 + half) * PEER_HALF, PEER_HALF), PEER_HALF), :]
            sc.append(jnp.dot(sub_ref[hh, half], qrows, precision=_HI, preferred_element_type=jnp.float32))

        def emit(res):
            c_ref[hh] = res[0]
            p1_ref[hh] = res[1]
            r2_ref[hh] = res[2].astype(r2_ref.dtype)
            p2_ref[hh] = res[3].astype(p2_ref.dtype)

        fast = _head_select(sc[0], sc[1], r1map_ref[...], r2map_ref[...], tie_exact=False)
        emit(fast)

        @pl.when(fast[4] > 0.0)
        def _():
            emit(_head_select(sc[0], sc[1], r1map_ref[...], r2map_ref[...], tie_exact=True))

        return carry

    lax.fori_loop(0, PEER_HEADS, head_step, 0)


def _peer_select_call(pqt, sub, te):
    d, n = pqt.shape
    r1map, r2map = _stair_maps(te)
    f32_shape = jax.ShapeDtypeStruct((PEER_HEADS, N_KEYS, n), jnp.float32)
    b16_shape = jax.ShapeDtypeStruct((PEER_HEADS, N_KEYS, n), MXU_DTYPE)
    ospec = pl.BlockSpec((PEER_HEADS, N_KEYS, te), lambda i: (0, 0, i))
    return pl.pallas_call(
        _peer_select_kernel,
        grid=(n // te,),
        in_specs=[pl.BlockSpec((d, te), lambda i: (0, i)),
                  pl.BlockSpec(sub.shape, lambda i: (0, 0, 0, 0)),
                  pl.BlockSpec(r1map.shape, lambda i: (0, 0)),
                  pl.BlockSpec(r2map.shape, lambda i: (0, 0))],
        out_specs=(ospec, ospec, ospec, ospec),
        out_shape=(f32_shape, f32_shape, b16_shape, b16_shape),
        compiler_params=_cparams(("parallel",)),
        name="peer_select",
    )(pqt, sub, r1map, r2map)


def _peer_dense_kernel(h2t_ref, u_ref, vt_ref, c_ref, p1_ref, r2_ref, p2_ref, x1_ref, ada_ref,
                       o_ref, acc_ref, g_ref, *, keys_per_step, n_chunks):
    j = pl.program_id(1)
    ec = u_ref.shape[0]

    @pl.when(j == 0)
    def _():
        acc_ref[...] = jnp.zeros_like(acc_ref)
        g_ref[1] = jnp.zeros(g_ref.shape[1:], g_ref.dtype)

    first_key = jnp.minimum(j, n_chunks - 1) * keys_per_step

    def step(slot):
        acc_ref[...] += jnp.dot(vt_ref[...], g_ref[1 - slot], preferred_element_type=jnp.float32)
        for sub in range(ec // PEER_SUB):
            lo = sub * PEER_SUB
            a = jnp.dot(u_ref[lo:lo + PEER_SUB, :], h2t_ref[...], preferred_element_type=jnp.float32)
            act = (0.5 * a * (1.0 + lax.erf(a * (2.0 ** -0.5)))).astype(g_ref.dtype)
            for kk in range(PEER_SUB // N_KEYS):
                i1 = first_key + (lo // N_KEYS + kk)
                w = None
                for hh in range(PEER_HEADS):
                    cb = c_ref[hh, pl.ds(i1, 1), :].astype(g_ref.dtype)
                    pb = p1_ref[hh, pl.ds(i1, 1), :].astype(g_ref.dtype)
                    term = jnp.where(r2_ref[hh] < cb, p2_ref[hh], jnp.zeros((), g_ref.dtype)) * pb
                    w = term if w is None else w + term
                g_ref[slot, lo + kk * N_KEYS:lo + (kk + 1) * N_KEYS, :] = w * act[kk * N_KEYS:(kk + 1) * N_KEYS, :]

    pl.when(j % 2 == 0)(functools.partial(step, 0))
    pl.when(j % 2 == 1)(functools.partial(step, 1))

    @pl.when(j == n_chunks)
    def _():
        gate2 = ada_ref[0, 5:6, :]
        o_ref[...] = x1_ref[...] + gate2 * acc_ref[...].T


def _peer_dense_call(h2t, u, vt, c_d, p1_d, r2_d, p2_d, x1, ada3, seq, tt, ec):
    d, n = h2t.shape
    n_chunks = u.shape[0] // ec
    tiles_per_seq = seq // tt
    kern = functools.partial(_peer_dense_kernel, keys_per_step=ec // N_KEYS, n_chunks=n_chunks)
    sel_spec = pl.BlockSpec((PEER_HEADS, N_KEYS, tt), lambda i, j: (0, 0, i))
    return pl.pallas_call(
        kern,
        grid=(n // tt, n_chunks + 1),
        in_specs=[pl.BlockSpec((d, tt), lambda i, j: (0, i)),
                  pl.BlockSpec((ec, d), lambda i, j: (jnp.minimum(j, n_chunks - 1), 0)),
                  pl.BlockSpec((d, ec), lambda i, j: (0, jnp.maximum(j - 1, 0))),
                  sel_spec, sel_spec, sel_spec, sel_spec,
                  pl.BlockSpec((tt, d), lambda i, j: (i, 0)),
                  pl.BlockSpec((1, 6, d), lambda i, j: (i // tiles_per_seq, 0, 0))],
        out_specs=pl.BlockSpec((tt, d), lambda i, j: (i, 0)),
        out_shape=jax.ShapeDtypeStruct((n, d), jnp.float32),
        scratch_shapes=[pltpu.VMEM((d, tt), jnp.float32), pltpu.VMEM((2, ec, tt), MXU_DTYPE)],
        compiler_params=_cparams(("parallel", "arbitrary")),
        name="peer_dense",
    )(h2t, u, vt, c_d, p1_d, r2_d, p2_d, x1, ada3)


def _block_diag_mean(width, group):
    idx = np.arange(width) // group
    return jnp.asarray((idx[:, None] == idx[None, :]).astype(np.float32) / group, dtype=jnp.bfloat16)


def _layer(x2, c, w_ada, b_ada, g_norm1, w_in, g_q, g_k, g_ik, w_pool_grp, s_pool, w_up_attn, w_up_pool,
           w_out, g_norm2, w_peer_q, peer_subkeys, peer_u, peer_v, bsz, seq):
    f32 = jnp.float32
    d = D_MODEL
    tm_b = min(512, seq)
    tq = min(512, seq)
    tm_d = min(256, seq)
    te = 256
    tt = min(512, seq)
    ec = 1024

    ada3 = _ada_call(c, w_ada, b_ada).reshape(bsz, 6, d)

    wa = jnp.pad(w_in[:, :_ZP0], ((0, 0), (0, _SLAB_A - _ZP0))).astype(MXU_DTYPE)
    wp = w_in[:, _ZP0:_ZG0].astype(MXU_DTYPE)
    wg = w_in[:, _ZG0:].astype(MXU_DTYPE)
    gcol = jnp.concatenate([jnp.tile(g_q, N_HEADS) * (HEAD_DIM ** -0.5), g_k, jnp.ones((_IK0 - _V0,), f32),
                            g_ik, jnp.ones((_SLAB_A - _IK0 - IDX_DIM,), f32)]).reshape(1, _SLAB_A)
    bd_a = _block_diag_mean(_IQ0, HEAD_DIM)
    bd_b = _block_diag_mean(_SLAB_A - _IK0, IDX_DIM)

    q, k, v, iq, ik, iw, zp = _inproj_call(x2, ada3, g_norm1.reshape(1, d), wa, wp, bd_a, bd_b, gcol, seq, tm_b)

    attn = _dsa_call(q, iq, iw, k, v, ik, bsz, seq, tq)

    x1, h2t, pqt = _mix_call(x2, ada3, g_norm1.reshape(1, d), g_norm2.reshape(1, d), attn, zp, wg,
                             w_pool_grp.astype(MXU_DTYPE), s_pool, w_up_attn.astype(MXU_DTYPE),
                             w_up_pool.astype(MXU_DTYPE), w_out.astype(MXU_DTYPE),
                             w_peer_q.T.astype(MXU_DTYPE), seq, tm_d)

    c_d, p1_d, r2_d, p2_d = _peer_select_call(pqt, peer_subkeys, te)

    return _peer_dense_call(h2t, peer_u.astype(MXU_DTYPE), peer_v.T.astype(MXU_DTYPE),
                            c_d, p1_d, r2_d, p2_d, x1, ada3, seq, tt, ec)


def kernel(x, c, w_ada, b_ada, g_norm1, w_in, g_q, g_k, g_ik, w_pool_grp, s_pool, w_up_attn, w_up_pool, w_out,
           g_norm2, w_peer_q, peer_subkeys, peer_u, peer_v):
    bsz, seq, d = x.shape
    x2 = x.reshape(bsz * seq, d)
    for layer in range(w_ada.shape[0]):
        x2 = _layer(x2, c, w_ada[layer], b_ada[layer], g_norm1[layer], w_in[layer], g_q[layer], g_k[layer],
                    g_ik[layer], w_pool_grp[layer], s_pool[layer], w_up_attn[layer], w_up_pool[layer],
                    w_out[layer], g_norm2[layer], w_peer_q[layer], peer_subkeys[layer], peer_u[layer],
                    peer_v[layer], bsz, seq)
    return x2.reshape(bsz, seq, d)
```

```python
import functools

import jax
import jax.numpy as jnp
import numpy as np
from jax import lax
from jax.experimental import pallas as pl
from jax.experimental.pallas import tpu as pltpu

D_MODEL = 1024
CHUNK = 64
N_HEADS = 8
HEAD_DIM = 64
ATTN_WIDTH = N_HEADS * HEAD_DIM
N_IDX_HEADS = 4
IDX_DIM = 64
TOPK_MAX = 256
DSA_SUB = 128
DSA_STRIP = 64
POOL_WINDOWS = (2, 4, 8, 16)
N_POOL_GROUPS = 4
POOL_WIDTH = 512
POOL_GROUP = POOL_WIDTH // N_POOL_GROUPS
PEER_HEADS = 8
PEER_HALF = 64
N_KEYS = 128
PEER_TOPK = 16
PEER_SUB = 256
EPS = 1e-6
POOL_HALO = 16

_Q0, _K0, _V0, _IQ0, _IK0, _IW0 = 0, 512, 576, 640, 896, 960
_SLAB_A = 1024
_ZP0 = 964
_ZG0 = 1476

MXU_DTYPE = jnp.bfloat16
VMEM_LIMIT = 56 * 1024 * 1024

_INT_MIN = -(2 ** 31)
_SOFTMAX_FLOOR = -1e30
_HI = lax.Precision.HIGHEST


def _cparams(sem, flags=None):
    return pltpu.CompilerParams(dimension_semantics=sem, vmem_limit_bytes=VMEM_LIMIT, flags=flags)


def _mm(a, b):
    return jnp.dot(a.astype(MXU_DTYPE), b.astype(MXU_DTYPE), preferred_element_type=jnp.float32)


def _mm_nt(a, b):
    return lax.dot_general(a.astype(MXU_DTYPE), b.astype(MXU_DTYPE), (((1,), (1,)), ((), ())),
                           preferred_element_type=jnp.float32)


def _rms_rows(x):
    return x * lax.rsqrt(jnp.mean(x * x, axis=-1, keepdims=True) + EPS)


def _group_mean_sq(z, bd):
    sq = z * z
    hi = sq.astype(jnp.bfloat16)
    lo = (sq - hi.astype(jnp.float32)).astype(jnp.bfloat16)
    return (jnp.dot(hi, bd, preferred_element_type=jnp.float32)
            + jnp.dot(lo, bd, preferred_element_type=jnp.float32))


def _ada_kernel(c_ref, w_ref, b_ref, o_ref):
    o_ref[...] = jnp.dot(c_ref[...], w_ref[...], precision=_HI,
                         preferred_element_type=jnp.float32) + b_ref[...]


def _ada_call(c, w, b):
    bsz, d = c.shape
    n_out = w.shape[1]
    tn = 1024
    return pl.pallas_call(
        _ada_kernel,
        grid=(n_out // tn,),
        in_specs=[pl.BlockSpec((bsz, d), lambda j: (0, 0)),
                  pl.BlockSpec((d, tn), lambda j: (0, j)),
                  pl.BlockSpec((1, tn), lambda j: (0, j))],
        out_specs=pl.BlockSpec((bsz, tn), lambda j: (0, j)),
        out_shape=jax.ShapeDtypeStruct((bsz, n_out), jnp.float32),
        compiler_params=_cparams(("arbitrary",)),
        name="ada",
    )(c, w, b.reshape(1, n_out))


def _inproj_kernel(x_ref, ada_ref, g1_ref, wa_ref, wp_ref, bd_a_ref, bd_b_ref, gcol_ref,
                   q_ref, k_ref, v_ref, iq_ref, ik_ref, iw_ref, zp_ref):
    x = x_ref[...]
    shift1 = ada_ref[0, 0:1, :]
    scale1 = ada_ref[0, 1:2, :]
    h = (_rms_rows(x) * g1_ref[...]) * (1.0 + scale1) + shift1
    hb = h.astype(MXU_DTYPE)
    z = jnp.dot(hb, wa_ref[...], preferred_element_type=jnp.float32)
    zp_ref[...] = jnp.dot(hb, wp_ref[...], preferred_element_type=jnp.float32)

    za = z[:, 0:_IQ0]
    na = za * lax.rsqrt(_group_mean_sq(za, bd_a_ref[...]) + EPS) * gcol_ref[:, 0:_IQ0]
    for hh in range(N_HEADS):
        q_ref[hh] = na[:, hh * HEAD_DIM:(hh + 1) * HEAD_DIM].astype(q_ref.dtype)
    k_ref[...] = na[:, _K0:_K0 + HEAD_DIM].astype(k_ref.dtype)
    lane = lax.broadcasted_iota(jnp.int32, (z.shape[0], 2 * HEAD_DIM), 1)
    v_ref[...] = jnp.where(lane >= HEAD_DIM, z[:, _K0:_K0 + 2 * HEAD_DIM],
                           jnp.where(lane == 0, 1.0, 0.0)).astype(v_ref.dtype)
    for hh in range(N_IDX_HEADS):
        iq_ref[hh] = (z[:, _IQ0 + hh * IDX_DIM:_IQ0 + (hh + 1) * IDX_DIM] * (IDX_DIM ** -0.5)).astype(iq_ref.dtype)
    zb = z[:, _IK0:_SLAB_A]
    nb = zb * lax.rsqrt(_group_mean_sq(zb, bd_b_ref[...]) + EPS) * gcol_ref[:, _IK0:_SLAB_A]
    ik_ref[...] = nb[:, 0:IDX_DIM].astype(ik_ref.dtype)
    iw_ref[...] = zb[:, IDX_DIM:IDX_DIM + N_IDX_HEADS] * (N_IDX_HEADS ** -0.5)


def _inproj_call(x2, ada3, g1, wa, wp, bd_a, bd_b, gcol, seq, tm):
    n, d = x2.shape
    tiles_per_seq = seq // tm
    f32 = jnp.float32
    out_shape = (
        jax.ShapeDtypeStruct((N_HEADS, n, HEAD_DIM), MXU_DTYPE),
        jax.ShapeDtypeStruct((n, HEAD_DIM), MXU_DTYPE),
        jax.ShapeDtypeStruct((n, 2 * HEAD_DIM), MXU_DTYPE),
        jax.ShapeDtypeStruct((N_IDX_HEADS, n, IDX_DIM), MXU_DTYPE),
        jax.ShapeDtypeStruct((n, IDX_DIM), MXU_DTYPE),
        jax.ShapeDtypeStruct((n, N_IDX_HEADS), f32),
        jax.ShapeDtypeStruct((n, POOL_WIDTH), f32),
    )
    const2 = lambda i: (0, 0)
    return pl.pallas_call(
        _inproj_kernel,
        grid=(n // tm,),
        in_specs=[pl.BlockSpec((tm, d), lambda i: (i, 0)),
                  pl.BlockSpec((1, 6, d), lambda i: (i // tiles_per_seq, 0, 0)),
                  pl.BlockSpec((1, d), const2),
                  pl.BlockSpec(wa.shape, const2),
                  pl.BlockSpec(wp.shape, const2),
                  pl.BlockSpec(bd_a.shape, const2),
                  pl.BlockSpec(bd_b.shape, const2),
                  pl.BlockSpec(gcol.shape, const2)],
        out_specs=(pl.BlockSpec((N_HEADS, tm, HEAD_DIM), lambda i: (0, i, 0)),
                   pl.BlockSpec((tm, HEAD_DIM), lambda i: (i, 0)),
                   pl.BlockSpec((tm, 2 * HEAD_DIM), lambda i: (i, 0)),
                   pl.BlockSpec((N_IDX_HEADS, tm, IDX_DIM), lambda i: (0, i, 0)),
                   pl.BlockSpec((tm, IDX_DIM), lambda i: (i, 0)),
                   pl.BlockSpec((tm, N_IDX_HEADS), lambda i: (i, 0)),
                   pl.BlockSpec((tm, POOL_WIDTH), lambda i: (i, 0))),
        out_shape=out_shape,
        compiler_params=_cparams(("parallel",)),
        name="inproj",
    )(x2, ada3, g1, wa, wp, bd_a, bd_b, gcol)


def _sortable_key(score):
    bits = lax.bitcast_convert_type(score + 0.0, jnp.int32)
    return bits ^ (lax.shift_right_arithmetic(bits, 31) & 0x7FFFFFFF)


def _dsa_kernel(q_ref, iq_ref, iw_ref, k_ref, v_ref, ik_ref, ut_ref, ones_ref, o_ref, key_ref, bias_ref,
                cand_ref, part_ref, *, n_sel, tq):
    blk = pl.program_id(1)
    n_groups = blk + 1
    grp = key_ref.shape[2]
    kf = float(n_sel)
    lanes = grp // 128
    q_chunk = (blk * tq + lax.broadcasted_iota(jnp.int32, (tq, grp), 0)) // CHUNK
    col = lax.broadcasted_iota(jnp.int32, (tq, grp), 1)
    iw = iw_ref[...]

    def build_keys(g, carry):
        ikg = ik_ref[pl.ds(pl.multiple_of(g * grp, grp), grp), :]
        score = None
        for hh in range(N_IDX_HEADS):
            term = jnp.maximum(_mm_nt(iq_ref[hh], ikg), 0.0) * iw[:, hh:hh + 1]
            score = term if score is None else score + term
        key_ref[g] = jnp.where((g * grp + col) // CHUNK <= q_chunk, _sortable_key(score), _INT_MIN)
        return carry

    lax.fori_loop(0, n_groups, build_keys, 0)

    def count(strictly_greater):
        def add_group(g, carry):
            for st in range(tq // DSA_STRIP):
                rows = slice(st * DSA_STRIP, (st + 1) * DSA_STRIP)
                cand = cand_ref[rows, :]
                part = part_ref[rows, :]
                for t in range(lanes):
                    key = key_ref[g, rows, t * 128:(t + 1) * 128]
                    part = part + jnp.where(key > cand if strictly_greater else key >= cand, 1.0, 0.0)
                part_ref[rows, :] = part
            return carry
        part_ref[...] = jnp.zeros_like(part_ref)
        lax.fori_loop(0, n_groups, add_group, 0)
        return jnp.sum(part_ref[...], axis=1, keepdims=True)

    cand_ref[...] = jnp.zeros_like(cand_ref)
    thr = jnp.where(count(False) >= kf, 0, _INT_MIN).astype(jnp.int32)

    def bit_step(i, thr):
        cand = thr | lax.shift_left(jnp.int32(1), 30 - i)
        cand_ref[...] = jnp.broadcast_to(cand, cand_ref.shape)
        return jnp.where(count(False) >= kf, cand, thr)

    thr = lax.fori_loop(0, 31, bit_step, thr)
    cand_ref[...] = jnp.broadcast_to(thr, cand_ref.shape)

    room = jnp.broadcast_to(kf - count(True), (tq, 128))

    def emit_bias(g, seen):
        thr_t = cand_ref[...]
        keys = [key_ref[g, :, t * 128:(t + 1) * 128] for t in range(lanes)]
        eqs = [key == thr_t for key in keys]
        eqb = jnp.concatenate([jnp.where(eq, 1.0, 0.0) for eq in eqs], axis=1).astype(jnp.bfloat16)
        prefix = seen + jnp.dot(eqb, ut_ref[...], preferred_element_type=jnp.float32)
        for t in range(lanes):
            tie_ok = eqs[t] & (prefix[:, t * 128:(t + 1) * 128] <= room)
            sel = ((keys[t] > thr_t) | tie_ok) & (keys[t] != _INT_MIN)
            bias_ref[g, :, t * 128:(t + 1) * 128] = jnp.where(sel, 0.0, -jnp.inf)
        return seen + jnp.dot(eqb, ones_ref[...], preferred_element_type=jnp.float32)

    lax.fori_loop(0, n_groups, emit_bias, jnp.zeros((tq, grp), jnp.float32))

    def sub_block(sub, carry):
        r0 = pl.multiple_of(sub * DSA_SUB, DSA_SUB)
        qs = q_ref[:, pl.ds(r0, DSA_SUB), :].reshape(N_HEADS * DSA_SUB, HEAD_DIM)

        def group_step(g, state):
            m, acc = state
            k0 = pl.multiple_of(g * grp, grp)
            s = _mm_nt(qs, k_ref[pl.ds(k0, grp), :]).reshape(N_HEADS, DSA_SUB, grp)
            s = s + bias_ref[g, pl.ds(r0, DSA_SUB), :][None]
            m_new = jnp.maximum(m, jnp.max(s, axis=2, keepdims=True))
            alpha = jnp.exp(m - m_new)
            p = jnp.exp((s - m_new).astype(MXU_DTYPE))
            pv = jnp.dot(p.reshape(N_HEADS * DSA_SUB, grp), v_ref[pl.ds(k0, grp), :],
                         preferred_element_type=jnp.float32)
            return m_new, alpha * acc + pv.reshape(N_HEADS, DSA_SUB, 2 * HEAD_DIM)

        init = (jnp.full((N_HEADS, DSA_SUB, 1), _SOFTMAX_FLOOR, jnp.float32),
                jnp.zeros((N_HEADS, DSA_SUB, 2 * HEAD_DIM), jnp.float32))
        _, acc = lax.fori_loop(0, n_groups, group_step, init)
        out = acc[:, :, HEAD_DIM:] / acc[:, :, 0:1]
        o_ref[pl.ds(r0, DSA_SUB), :] = jnp.concatenate([out[hh] for hh in range(N_HEADS)], axis=1).astype(o_ref.dtype)
        return carry

    lax.fori_loop(0, tq // DSA_SUB, sub_block, 0)


def _dsa_call(q, iq, iw, k, v, ik, bsz, seq, tq):
    n = k.shape[0]
    n_sel = min(TOPK_MAX, seq // 4)
    nq = seq // tq
    ut = jnp.asarray(np.triu(np.ones((tq, tq), np.float32)), dtype=jnp.bfloat16)
    ones = jnp.ones((tq, tq), jnp.bfloat16)
    kern = functools.partial(_dsa_kernel, n_sel=n_sel, tq=tq)
    per_batch = lambda b, j: (b, 0)
    const2 = lambda b, j: (0, 0)
    return pl.pallas_call(
        kern,
        grid=(bsz, nq),
        in_specs=[pl.BlockSpec((N_HEADS, tq, HEAD_DIM), lambda b, j: (0, b * nq + j, 0)),
                  pl.BlockSpec((N_IDX_HEADS, tq, IDX_DIM), lambda b, j: (0, b * nq + j, 0)),
                  pl.BlockSpec((tq, N_IDX_HEADS), lambda b, j: (b * nq + j, 0)),
                  pl.BlockSpec((seq, HEAD_DIM), per_batch),
                  pl.BlockSpec((seq, 2 * HEAD_DIM), per_batch),
                  pl.BlockSpec((seq, IDX_DIM), per_batch),
                  pl.BlockSpec(ut.shape, const2),
                  pl.BlockSpec(ones.shape, const2)],
        out_specs=pl.BlockSpec((tq, ATTN_WIDTH), lambda b, j: (b * nq + j, 0)),
        out_shape=jax.ShapeDtypeStruct((n, ATTN_WIDTH), MXU_DTYPE),
        scratch_shapes=[pltpu.VMEM((nq, tq, tq), jnp.int32), pltpu.VMEM((nq, tq, tq), jnp.float32),
                        pltpu.VMEM((tq, 128), jnp.int32), pltpu.VMEM((tq, 128), jnp.float32)],
        compiler_params=_cparams(("parallel", "parallel")),
        name="dsa",
    )(q, iq, iw, k, v, ik, ut, ones)


def _mix_kernel(x_ref, ada_ref, g1_ref, g2_ref, attn_ref, zp_ref, halo_ref, wg_ref, wgrp_ref, spool_ref,
                wua_ref, wup_ref, wo_ref, wqt_ref, x1_ref, h2t_ref, pqt_ref, *, tiles_per_seq, tm):
    i = pl.program_id(0)
    x = x_ref[...]
    shift1 = ada_ref[0, 0:1, :]
    scale1 = ada_ref[0, 1:2, :]
    gate1 = ada_ref[0, 2:3, :]
    shift2 = ada_ref[0, 3:4, :]
    scale2 = ada_ref[0, 4:5, :]
    h = (_rms_rows(x) * g1_ref[...]) * (1.0 + scale1) + shift1
    zg = jnp.dot(h.astype(MXU_DTYPE), wg_ref[...], preferred_element_type=jnp.float32)
    gates = jax.nn.sigmoid(zg)

    first_in_seq = (i % tiles_per_seq) == 0
    halo = jnp.where(first_in_seq, 0.0, halo_ref[...])
    ext = jnp.concatenate([halo, zp_ref[...]], axis=0)
    t_seq = ((i % tiles_per_seq) * tm + lax.broadcasted_iota(jnp.int32, (tm, 1), 0)).astype(jnp.float32)
    run = ext
    pooled = []
    for g, w in enumerate(POOL_WINDOWS):
        run = run + pltpu.roll(run, w // 2, axis=0)
        cnt = jnp.minimum(t_seq + 1.0, float(w))
        lo, hi = g * POOL_GROUP, (g + 1) * POOL_GROUP
        mean = run[POOL_HALO:, lo:hi] / cnt
        mixed = (mean - ext[POOL_HALO:, lo:hi])
        pooled.append(_mm(mixed, wgrp_ref[g]) * spool_ref[g:g + 1, :])
    pool_out = jnp.concatenate(pooled, axis=1)

    y_attn = jnp.dot(attn_ref[...], wua_ref[...], preferred_element_type=jnp.float32)
    y_pool = _mm(pool_out, wup_ref[...])
    merged = gates[:, 0:D_MODEL] * y_attn + gates[:, D_MODEL:] * y_pool
    x1 = x + gate1 * _mm(merged, wo_ref[...])
    x1_ref[...] = x1

    h2 = (_rms_rows(x1) * g2_ref[...]) * (1.0 + scale2) + shift2
    h2t = h2.T.astype(MXU_DTYPE)
    h2t_ref[...] = h2t
    pqt_ref[...] = jnp.dot(wqt_ref[...], h2t, preferred_element_type=jnp.float32)


def _mix_call(x2, ada3, g1, g2, attn, zp, wg, wgrp, spool, wua, wup, wo, wqt, seq, tm):
    n, d = x2.shape
    tiles_per_seq = seq // tm
    halo_per_tile = tm // POOL_HALO
    kern = functools.partial(_mix_kernel, tiles_per_seq=tiles_per_seq, tm=tm)
    const2 = lambda i: (0, 0)
    const3 = lambda i: (0, 0, 0)
    return pl.pallas_call(
        kern,
        grid=(n // tm,),
        in_specs=[pl.BlockSpec((tm, d), lambda i: (i, 0)),
                  pl.BlockSpec((1, 6, d), lambda i: (i // tiles_per_seq, 0, 0)),
                  pl.BlockSpec((1, d), const2),
                  pl.BlockSpec((1, d), const2),
                  pl.BlockSpec((tm, ATTN_WIDTH), lambda i: (i, 0)),
                  pl.BlockSpec((tm, POOL_WIDTH), lambda i: (i, 0)),
                  pl.BlockSpec((POOL_HALO, POOL_WIDTH), lambda i: (jnp.maximum(i * halo_per_tile - 1, 0), 0)),
                  pl.BlockSpec(wg.shape, const2),
                  pl.BlockSpec(wgrp.shape, const3),
                  pl.BlockSpec(spool.shape, const2),
                  pl.BlockSpec(wua.shape, const2),
                  pl.BlockSpec(wup.shape, const2),
                  pl.BlockSpec(wo.shape, const2),
                  pl.BlockSpec(wqt.shape, const2)],
        out_specs=(pl.BlockSpec((tm, d), lambda i: (i, 0)),
                   pl.BlockSpec((d, tm), lambda i: (0, i)),
                   pl.BlockSpec((d, tm), lambda i: (0, i))),
        out_shape=(jax.ShapeDtypeStruct((n, d), jnp.float32),
                   jax.ShapeDtypeStruct((d, n), MXU_DTYPE),
                   jax.ShapeDtypeStruct((d, n), jnp.float32)),
        compiler_params=_cparams(("parallel",)),
        name="mix",
    )(x2, ada3, g1, g2, attn, zp, zp, wg, wgrp, spool, wua, wup, wo, wqt)


_STAIR = [(r1, r2) for r1 in range(PEER_TOPK) for r2 in range(PEER_TOPK // (r1 + 1))]
_STAIR_ROWS = -(-len(_STAIR) // 8) * 8


def _stair_maps(te):
    r1 = np.full((_STAIR_ROWS, 1), -1.0, np.float32)
    r2 = np.full((_STAIR_ROWS, 1), -1.0, np.float32)
    for row, (a, b) in enumerate(_STAIR):
        r1[row, 0], r2[row, 0] = a, b
    return jnp.asarray(np.tile(r1, (1, te))), jnp.asarray(np.tile(r2, (1, te)))


def _extract_top(s, n_rounds, tie_exact):
    r, t = s.shape
    rows = lax.broadcasted_iota(jnp.int32, (r, t), 0).astype(jnp.float32)
    rank = jnp.full((r, t), float(n_rounds), jnp.float32)
    vals = []
    for rnd in range(n_rounds):
        m = jnp.max(s, axis=0, keepdims=True)
        hit = s == m
        if tie_exact:
            first = jnp.min(jnp.where(hit, rows, float(r)), axis=0, keepdims=True)
            hit = rows == first
        rank = jnp.where(hit, float(rnd), rank)
        s = jnp.where(hit, -jnp.inf, s)
        vals.append(m)
    return vals, rank


def _count_rows(mask):
    return jnp.sum(jnp.where(mask, 1.0, 0.0), axis=0, keepdims=True)


def _head_select(sc0, sc1, r1map, r2map, tie_exact):
    kf = float(PEER_TOPK)
    vals0, rk0 = _extract_top(sc0, PEER_TOPK, tie_exact)
    vals1, rk1 = _extract_top(sc1, PEER_TOPK, tie_exact)
    a1 = jnp.full(r1map.shape, -jnp.inf, jnp.float32)
    a2 = jnp.zeros(r2map.shape, jnp.float32)
    for r in range(PEER_TOPK):
        a1 = jnp.where(r1map == float(r), vals0[r], a1)
        a2 = jnp.where(r2map == float(r), vals1[r], a2)
    best, rkc = _extract_top(a1 + a2, PEER_TOPK, tie_exact)
    picked = rkc < kf
    c_dense = jnp.zeros(sc0.shape, jnp.float32)
    for r in range(PEER_TOPK):
        c_dense = jnp.where(rk0 == float(r), _count_rows(picked & (r1map == float(r))), c_dense)
    zsum = jnp.ones_like(best[0])
    for j in range(1, PEER_TOPK):
        zsum = zsum + jnp.exp(best[j] - best[0])
    p1 = jnp.where(rk0 < kf, jnp.exp(sc0 - vals0[0]), 0.0) / zsum
    p2 = jnp.where(rk1 < kf, jnp.exp(sc1 - vals1[0]), 0.0)
    bad = (_count_rows(rk0 < kf) != kf) | (_count_rows(rk1 < kf) != kf) | (_count_rows(picked) != kf)
    return c_dense, p1, rk1, p2, jnp.sum(jnp.where(bad, 1.0, 0.0))


def _peer_select_kernel(pqt_ref, sub_ref, r1map_ref, r2map_ref, c_ref, p1_ref, r2_ref, p2_ref):
    def head_step(hh, carry):
        sc = []
        for half in range(2):
            qrows = pqt_ref[pl.ds(pl.multiple_of((hh * 2 + half) * PEER_HALF, PEER_HALF), PEER_HALF), :]
            sc.append(jnp.dot(sub_ref[hh, half], qrows, precision=_HI, preferred_element_type=jnp.float32))

        def emit(res):
            c_ref[hh] = res[0]
            p1_ref[hh] = res[1]
            r2_ref[hh] = res[2].astype(r2_ref.dtype)
            p2_ref[hh] = res[3].astype(p2_ref.dtype)

        fast = _head_select(sc[0], sc[1], r1map_ref[...], r2map_ref[...], tie_exact=False)
        emit(fast)

        @pl.when(fast[4] > 0.0)
        def _():
            emit(_head_select(sc[0], sc[1], r1map_ref[...], r2map_ref[...], tie_exact=True))

        return carry

    lax.fori_loop(0, PEER_HEADS, head_step, 0)


def _peer_select_call(pqt, sub, te):
    d, n = pqt.shape
    r1map, r2map = _stair_maps(te)
    f32_shape = jax.ShapeDtypeStruct((PEER_HEADS, N_KEYS, n), jnp.float32)
    b16_shape = jax.ShapeDtypeStruct((PEER_HEADS, N_KEYS, n), MXU_DTYPE)
    ospec = pl.BlockSpec((PEER_HEADS, N_KEYS, te), lambda i: (0, 0, i))
    return pl.pallas_call(
        _peer_select_kernel,
        grid=(n // te,),
        in_specs=[pl.BlockSpec((d, te), lambda i: (0, i)),
                  pl.BlockSpec(sub.shape, lambda i: (0, 0, 0, 0)),
                  pl.BlockSpec(r1map.shape, lambda i: (0, 0)),
                  pl.BlockSpec(r2map.shape, lambda i: (0, 0))],
        out_specs=(ospec, ospec, ospec, ospec),
        out_shape=(f32_shape, f32_shape, b16_shape, b16_shape),
        compiler_params=_cparams(("parallel",)),
        name="peer_select",
    )(pqt, sub, r1map, r2map)


def _peer_dense_kernel(h2t_ref, u_ref, vt_ref, c_ref, p1_ref, r2_ref, p2_ref, x1_ref, ada_ref,
                       o_ref, acc_ref, g_even_ref, g_odd_ref, w_ref, *, keys_per_step, n_chunks):
    j = pl.program_id(1)
    ec = u_ref.shape[0]
    gdt = g_even_ref.dtype

    @pl.when(j == 0)
    def _():
        acc_ref[...] = jnp.zeros_like(acc_ref)
        g_odd_ref[...] = jnp.zeros_like(g_odd_ref)

    first_key = jnp.minimum(j, n_chunks - 1) * keys_per_step

    def step(g_new_ref, g_prev_ref):
        n_sub = ec // PEER_SUB
        rows_per_sub = acc_ref.shape[0] // n_sub
        for sub in range(n_sub):
            lo = sub * PEER_SUB
            for kk in range(PEER_SUB // N_KEYS):
                i1 = first_key + (lo // N_KEYS + kk)
                w = None
                for hh in range(PEER_HEADS):
                    cb = c_ref[hh, pl.ds(i1, 1), :].astype(gdt)
                    pb = p1_ref[hh, pl.ds(i1, 1), :].astype(gdt)
                    term = jnp.where(r2_ref[hh] < cb, p2_ref[hh], jnp.zeros((), gdt)) * pb
                    w = term if w is None else w + term
                w_ref[lo + kk * N_KEYS:lo + (kk + 1) * N_KEYS, :] = w
            rows = slice(sub * rows_per_sub, (sub + 1) * rows_per_sub)
            acc_ref[rows, :] += jnp.dot(vt_ref[rows, :], g_prev_ref[...], preferred_element_type=jnp.float32)
            a = jnp.dot(u_ref[lo:lo + PEER_SUB, :], h2t_ref[...], preferred_element_type=jnp.float32)
            a = a.astype(gdt)
            act = 0.5 * a * (1.0 + lax.erf(a * (2.0 ** -0.5)))
            g_new_ref[lo:lo + PEER_SUB, :] = w_ref[lo:lo + PEER_SUB, :] * act

    pl.when(j % 2 == 0)(functools.partial(step, g_even_ref, g_odd_ref))
    pl.when(j % 2 == 1)(functools.partial(step, g_odd_ref, g_even_ref))

    @pl.when(j == n_chunks)
    def _():
        gate2 = ada_ref[0, 5:6, :]
        o_ref[...] = x1_ref[...] + gate2 * acc_ref[...].T


def _peer_dense_call(h2t, u, vt, c_d, p1_d, r2_d, p2_d, x1, ada3, seq, tt, ec):
    d, n = h2t.shape
    n_chunks = u.shape[0] // ec
    tiles_per_seq = seq // tt
    kern = functools.partial(_peer_dense_kernel, keys_per_step=ec // N_KEYS, n_chunks=n_chunks)
    sel_spec = pl.BlockSpec((PEER_HEADS, N_KEYS, tt), lambda i, j: (0, 0, i))
    return pl.pallas_call(
        kern,
        grid=(n // tt, n_chunks + 1),
        in_specs=[pl.BlockSpec((d, tt), lambda i, j: (0, i)),
                  pl.BlockSpec((ec, d), lambda i, j: (jnp.minimum(j, n_chunks - 1), 0)),
                  pl.BlockSpec((d, ec), lambda i, j: (0, jnp.maximum(j - 1, 0))),
                  sel_spec, sel_spec, sel_spec, sel_spec,
                  pl.BlockSpec((tt, d), lambda i, j: (i, 0)),
                  pl.BlockSpec((1, 6, d), lambda i, j: (i // tiles_per_seq, 0, 0))],
        out_specs=pl.BlockSpec((tt, d), lambda i, j: (i, 0)),
        out_shape=jax.ShapeDtypeStruct((n, d), jnp.float32),
        scratch_shapes=[pltpu.VMEM((d, tt), jnp.float32), pltpu.VMEM((ec, tt), MXU_DTYPE),
                        pltpu.VMEM((ec, tt), MXU_DTYPE), pltpu.VMEM((ec, tt), MXU_DTYPE)],
        compiler_params=_cparams(("parallel", "arbitrary")),
        name="peer_dense",
    )(h2t, u, vt, c_d, p1_d, r2_d, p2_d, x1, ada3)


def _block_diag_mean(width, group):
    idx = np.arange(width) // group
    return jnp.asarray((idx[:, None] == idx[None, :]).astype(np.float32) / group, dtype=jnp.bfloat16)


def _layer(x2, c, w_ada, b_ada, g_norm1, w_in, g_q, g_k, g_ik, w_pool_grp, s_pool, w_up_attn, w_up_pool,
           w_out, g_norm2, w_peer_q, peer_subkeys, peer_u, peer_v, bsz, seq):
    f32 = jnp.float32
    d = D_MODEL
    tm_b = min(512, seq)
    tq = min(512, seq)
    tm_d = min(256, seq)
    te = 256
    tt = min(512, seq)
    ec = 1024

    ada3 = _ada_call(c, w_ada, b_ada).reshape(bsz, 6, d)

    wa = jnp.pad(w_in[:, :_ZP0], ((0, 0), (0, _SLAB_A - _ZP0))).astype(MXU_DTYPE)
    wp = w_in[:, _ZP0:_ZG0].astype(MXU_DTYPE)
    wg = w_in[:, _ZG0:].astype(MXU_DTYPE)
    gcol = jnp.concatenate([jnp.tile(g_q, N_HEADS) * (HEAD_DIM ** -0.5), g_k, jnp.ones((_IK0 - _V0,), f32),
                            g_ik, jnp.ones((_SLAB_A - _IK0 - IDX_DIM,), f32)]).reshape(1, _SLAB_A)
    bd_a = _block_diag_mean(_IQ0, HEAD_DIM)
    bd_b = _block_diag_mean(_SLAB_A - _IK0, IDX_DIM)

    q, k, v, iq, ik, iw, zp = _inproj_call(x2, ada3, g_norm1.reshape(1, d), wa, wp, bd_a, bd_b, gcol, seq, tm_b)

    attn = _dsa_call(q, iq, iw, k, v, ik, bsz, seq, tq)

    x1, h2t, pqt = _mix_call(x2, ada3, g_norm1.reshape(1, d), g_norm2.reshape(1, d), attn, zp, wg,
                             w_pool_grp.astype(MXU_DTYPE), s_pool, w_up_attn.astype(MXU_DTYPE),
                             w_up_pool.astype(MXU_DTYPE), w_out.astype(MXU_DTYPE),
                             w_peer_q.T.astype(MXU_DTYPE), seq, tm_d)

    c_d, p1_d, r2_d, p2_d = _peer_select_call(pqt, peer_subkeys, te)

    return _peer_dense_call(h2t, peer_u.astype(MXU_DTYPE), peer_v.T.astype(MXU_DTYPE),
                            c_d, p1_d, r2_d, p2_d, x1, ada3, seq, tt, ec)


def kernel(x, c, w_ada, b_ada, g_norm1, w_in, g_q, g_k, g_ik, w_pool_grp, s_pool, w_up_attn, w_up_pool, w_out,
           g_norm2, w_peer_q, peer_subkeys, peer_u, peer_v):
    bsz, seq, d = x.shape
    x2 = x.reshape(bsz * seq, d)
    for layer in range(w_ada.shape[0]):
        x2 = _layer(x2, c, w_ada[layer], b_ada[layer], g_norm1[layer], w_in[layer], g_q[layer], g_k[layer],
                    g_ik[layer], w_pool_grp[layer], s_pool[layer], w_up_attn[layer], w_up_pool[layer],
                    w_out[layer], g_norm2[layer], w_peer_q[layer], peer_subkeys[layer], peer_u[layer],
                    peer_v[layer], bsz, seq)
    return x2.reshape(bsz, seq, d)
```

```python
import functools

import jax
import jax.numpy as jnp
import numpy as np
from jax import lax
from jax.experimental import pallas as pl
from jax.experimental.pallas import tpu as pltpu

D_MODEL = 1024
CHUNK = 64
N_HEADS = 8
HEAD_DIM = 64
ATTN_WIDTH = N_HEADS * HEAD_DIM
N_IDX_HEADS = 4
IDX_DIM = 64
TOPK_MAX = 256
DSA_SUB = 128
DSA_STRIP = 64
DSA_PARTS = 2
POOL_WINDOWS = (2, 4, 8, 16)
N_POOL_GROUPS = 4
POOL_WIDTH = 512
POOL_GROUP = POOL_WIDTH // N_POOL_GROUPS
PEER_HEADS = 8
PEER_HALF = 64
N_KEYS = 128
PEER_TOPK = 16
PEER_SUB = 256
EPS = 1e-6
POOL_HALO = 16

_Q0, _K0, _V0, _IQ0, _IK0, _IW0 = 0, 512, 576, 640, 896, 960
_SLAB_A = 1024
_ZP0 = 964
_ZG0 = 1476

MXU_DTYPE = jnp.bfloat16
VMEM_LIMIT = 56 * 1024 * 1024

_INT_MIN = -(2 ** 31)
_SOFTMAX_FLOOR = -1e30
_HI = lax.Precision.HIGHEST


def _cparams(sem, flags=None):
    return pltpu.CompilerParams(dimension_semantics=sem, vmem_limit_bytes=VMEM_LIMIT, flags=flags)


def _mm(a, b):
    return jnp.dot(a.astype(MXU_DTYPE), b.astype(MXU_DTYPE), preferred_element_type=jnp.float32)


def _mm_nt(a, b):
    return lax.dot_general(a.astype(MXU_DTYPE), b.astype(MXU_DTYPE), (((1,), (1,)), ((), ())),
                           preferred_element_type=jnp.float32)


def _rms_rows(x):
    return x * lax.rsqrt(jnp.mean(x * x, axis=-1, keepdims=True) + EPS)


def _group_mean_sq(z, bd):
    sq = z * z
    hi = sq.astype(jnp.bfloat16)
    lo = (sq - hi.astype(jnp.float32)).astype(jnp.bfloat16)
    return (jnp.dot(hi, bd, preferred_element_type=jnp.float32)
            + jnp.dot(lo, bd, preferred_element_type=jnp.float32))


def _ada_kernel(c_ref, w_ref, b_ref, o_ref):
    o_ref[...] = jnp.dot(c_ref[...], w_ref[...], precision=_HI,
                         preferred_element_type=jnp.float32) + b_ref[...]


def _ada_call(c, w, b):
    bsz, d = c.shape
    n_out = w.shape[1]
    tn = 1024
    return pl.pallas_call(
        _ada_kernel,
        grid=(n_out // tn,),
        in_specs=[pl.BlockSpec((bsz, d), lambda j: (0, 0)),
                  pl.BlockSpec((d, tn), lambda j: (0, j)),
                  pl.BlockSpec((1, tn), lambda j: (0, j))],
        out_specs=pl.BlockSpec((bsz, tn), lambda j: (0, j)),
        out_shape=jax.ShapeDtypeStruct((bsz, n_out), jnp.float32),
        compiler_params=_cparams(("arbitrary",)),
        name="ada",
    )(c, w, b.reshape(1, n_out))


def _inproj_kernel(x_ref, ada_ref, g1_ref, wa_ref, wp_ref, bd_a_ref, bd_b_ref, gcol_ref,
                   q_ref, k_ref, v_ref, iq_ref, ik_ref, iw_ref, zp_ref):
    x = x_ref[...]
    shift1 = ada_ref[0, 0:1, :]
    scale1 = ada_ref[0, 1:2, :]
    h = (_rms_rows(x) * g1_ref[...]) * (1.0 + scale1) + shift1
    hb = h.astype(MXU_DTYPE)
    z = jnp.dot(hb, wa_ref[...], preferred_element_type=jnp.float32)
    zp_ref[...] = jnp.dot(hb, wp_ref[...], preferred_element_type=jnp.float32)

    za = z[:, 0:_IQ0]
    na = za * lax.rsqrt(_group_mean_sq(za, bd_a_ref[...]) + EPS) * gcol_ref[:, 0:_IQ0]
    for hh in range(N_HEADS):
        q_ref[hh] = na[:, hh * HEAD_DIM:(hh + 1) * HEAD_DIM].astype(q_ref.dtype)
    k_ref[...] = na[:, _K0:_K0 + HEAD_DIM].astype(k_ref.dtype)
    lane = lax.broadcasted_iota(jnp.int32, (z.shape[0], 2 * HEAD_DIM), 1)
    v_ref[...] = jnp.where(lane >= HEAD_DIM, z[:, _K0:_K0 + 2 * HEAD_DIM],
                           jnp.where(lane == 0, 1.0, 0.0)).astype(v_ref.dtype)
    for hh in range(N_IDX_HEADS):
        iq_ref[hh] = (z[:, _IQ0 + hh * IDX_DIM:_IQ0 + (hh + 1) * IDX_DIM] * (IDX_DIM ** -0.5)).astype(iq_ref.dtype)
    zb = z[:, _IK0:_SLAB_A]
    nb = zb * lax.rsqrt(_group_mean_sq(zb, bd_b_ref[...]) + EPS) * gcol_ref[:, _IK0:_SLAB_A]
    ik_ref[...] = nb[:, 0:IDX_DIM].astype(ik_ref.dtype)
    iw_ref[...] = zb[:, IDX_DIM:IDX_DIM + N_IDX_HEADS] * (N_IDX_HEADS ** -0.5)


def _inproj_call(x2, ada3, g1, wa, wp, bd_a, bd_b, gcol, seq, tm):
    n, d = x2.shape
    tiles_per_seq = seq // tm
    f32 = jnp.float32
    out_shape = (
        jax.ShapeDtypeStruct((N_HEADS, n, HEAD_DIM), MXU_DTYPE),
        jax.ShapeDtypeStruct((n, HEAD_DIM), MXU_DTYPE),
        jax.ShapeDtypeStruct((n, 2 * HEAD_DIM), MXU_DTYPE),
        jax.ShapeDtypeStruct((N_IDX_HEADS, n, IDX_DIM), MXU_DTYPE),
        jax.ShapeDtypeStruct((n, IDX_DIM), MXU_DTYPE),
        jax.ShapeDtypeStruct((n, N_IDX_HEADS), f32),
        jax.ShapeDtypeStruct((n, POOL_WIDTH), f32),
    )
    const2 = lambda i: (0, 0)
    return pl.pallas_call(
        _inproj_kernel,
        grid=(n // tm,),
        in_specs=[pl.BlockSpec((tm, d), lambda i: (i, 0)),
                  pl.BlockSpec((1, 6, d), lambda i: (i // tiles_per_seq, 0, 0)),
                  pl.BlockSpec((1, d), const2),
                  pl.BlockSpec(wa.shape, const2),
                  pl.BlockSpec(wp.shape, const2),
                  pl.BlockSpec(bd_a.shape, const2),
                  pl.BlockSpec(bd_b.shape, const2),
                  pl.BlockSpec(gcol.shape, const2)],
        out_specs=(pl.BlockSpec((N_HEADS, tm, HEAD_DIM), lambda i: (0, i, 0)),
                   pl.BlockSpec((tm, HEAD_DIM), lambda i: (i, 0)),
                   pl.BlockSpec((tm, 2 * HEAD_DIM), lambda i: (i, 0)),
                   pl.BlockSpec((N_IDX_HEADS, tm, IDX_DIM), lambda i: (0, i, 0)),
                   pl.BlockSpec((tm, IDX_DIM), lambda i: (i, 0)),
                   pl.BlockSpec((tm, N_IDX_HEADS), lambda i: (i, 0)),
                   pl.BlockSpec((tm, POOL_WIDTH), lambda i: (i, 0))),
        out_shape=out_shape,
        compiler_params=_cparams(("parallel",)),
        name="inproj",
    )(x2, ada3, g1, wa, wp, bd_a, bd_b, gcol)


def _sortable_key(score):
    bits = lax.bitcast_convert_type(score + 0.0, jnp.int32)
    return bits ^ (lax.shift_right_arithmetic(bits, 31) & 0x7FFFFFFF)


def _dsa_kernel(q_ref, iq_ref, iw_ref, k_ref, v_ref, ik_ref, ut_ref, ones_ref, o_ref, key_ref, bias_ref,
                cand_ref, part_ref, *, n_sel, tq):
    blk = pl.program_id(1)
    n_groups = blk + 1
    grp = key_ref.shape[2]
    kf = float(n_sel)
    lanes = grp // 128
    q_chunk = (blk * tq + lax.broadcasted_iota(jnp.int32, (tq, grp), 0)) // CHUNK
    col = lax.broadcasted_iota(jnp.int32, (tq, grp), 1)
    iw = iw_ref[...]

    def build_keys(g, carry):
        ikg = ik_ref[pl.ds(pl.multiple_of(g * grp, grp), grp), :]
        score = None
        for hh in range(N_IDX_HEADS):
            term = jnp.maximum(_mm_nt(iq_ref[hh], ikg), 0.0) * iw[:, hh:hh + 1]
            score = term if score is None else score + term
        key_ref[g] = jnp.where((g * grp + col) // CHUNK <= q_chunk, _sortable_key(score), _INT_MIN)
        return carry

    lax.fori_loop(0, n_groups, build_keys, 0)

    def find_threshold(n):
        part_rows = tq // DSA_PARTS

        def count(part, cand, strictly_greater):
            sums = []
            for st in range(part_rows // DSA_STRIP):
                r0 = part * part_rows + st * DSA_STRIP
                c = jnp.broadcast_to(cand[st * DSA_STRIP:(st + 1) * DSA_STRIP], (DSA_STRIP, 128))
                acc = jnp.zeros((DSA_STRIP, 128), jnp.float32)
                for g in range(n):
                    for t in range(lanes):
                        key = key_ref[g, r0:r0 + DSA_STRIP, t * 128:(t + 1) * 128]
                        acc = acc + jnp.where(key > c if strictly_greater else key >= c, 1.0, 0.0)
                sums.append(acc)
            return jnp.sum(jnp.concatenate(sums, axis=0), axis=1, keepdims=True)

        zero = jnp.zeros((part_rows, 1), jnp.int32)
        start = tuple(jnp.where(count(part, zero, False) >= kf, 0, _INT_MIN).astype(jnp.int32)
                      for part in range(DSA_PARTS))

        def bit_step(i, thrs):
            bit = lax.shift_left(jnp.int32(1), 30 - i)
            return tuple(jnp.where(count(part, thrs[part] | bit, False) >= kf, thrs[part] | bit, thrs[part])
                         for part in range(DSA_PARTS))

        thrs = lax.fori_loop(0, 31, bit_step, start)
        for part in range(DSA_PARTS):
            rows = slice(part * part_rows, (part + 1) * part_rows)
            cand_ref[rows, :] = jnp.broadcast_to(thrs[part], (part_rows, 128))
            part_ref[rows, :] = jnp.broadcast_to(kf - count(part, thrs[part], True), (part_rows, 128))

    for n in range(1, key_ref.shape[0] + 1):
        pl.when(n_groups == n)(functools.partial(find_threshold, n))

    def emit_bias(g, seen):
        thr_t = cand_ref[...]
        room = part_ref[...]
        keys = [key_ref[g, :, t * 128:(t + 1) * 128] for t in range(lanes)]
        eqs = [key == thr_t for key in keys]
        eqb = jnp.concatenate([jnp.where(eq, 1.0, 0.0) for eq in eqs], axis=1).astype(jnp.bfloat16)
        prefix = seen + jnp.dot(eqb, ut_ref[...], preferred_element_type=jnp.float32)
        for t in range(lanes):
            tie_ok = eqs[t] & (prefix[:, t * 128:(t + 1) * 128] <= room)
            sel = ((keys[t] > thr_t) | tie_ok) & (keys[t] != _INT_MIN)
            bias_ref[g, :, t * 128:(t + 1) * 128] = jnp.where(sel, 0.0, -jnp.inf)
        return seen + jnp.dot(eqb, ones_ref[...], preferred_element_type=jnp.float32)

    lax.fori_loop(0, n_groups, emit_bias, jnp.zeros((tq, grp), jnp.float32))

    def sub_block(sub, carry):
        r0 = pl.multiple_of(sub * DSA_SUB, DSA_SUB)
        qs = q_ref[:, pl.ds(r0, DSA_SUB), :].reshape(N_HEADS * DSA_SUB, HEAD_DIM)

        def group_step(g, state):
            m, acc = state
            k0 = pl.multiple_of(g * grp, grp)
            s = _mm_nt(qs, k_ref[pl.ds(k0, grp), :]).reshape(N_HEADS, DSA_SUB, grp)
            s = s + bias_ref[g, pl.ds(r0, DSA_SUB), :][None]
            m_new = jnp.maximum(m, jnp.max(s, axis=2, keepdims=True))
            alpha = jnp.exp(m - m_new)
            p = jnp.exp((s - m_new).astype(MXU_DTYPE))
            pv = jnp.dot(p.reshape(N_HEADS * DSA_SUB, grp), v_ref[pl.ds(k0, grp), :],
                         preferred_element_type=jnp.float32)
            return m_new, alpha * acc + pv.reshape(N_HEADS, DSA_SUB, 2 * HEAD_DIM)

        init = (jnp.full((N_HEADS, DSA_SUB, 1), _SOFTMAX_FLOOR, jnp.float32),
                jnp.zeros((N_HEADS, DSA_SUB, 2 * HEAD_DIM), jnp.float32))
        _, acc = lax.fori_loop(0, n_groups, group_step, init)
        out = acc[:, :, HEAD_DIM:] / acc[:, :, 0:1]
        o_ref[pl.ds(r0, DSA_SUB), :] = jnp.concatenate([out[hh] for hh in range(N_HEADS)], axis=1).astype(o_ref.dtype)
        return carry

    lax.fori_loop(0, tq // DSA_SUB, sub_block, 0)


def _dsa_call(q, iq, iw, k, v, ik, bsz, seq, tq):
    n = k.shape[0]
    n_sel = min(TOPK_MAX, seq // 4)
    nq = seq // tq
    ut = jnp.asarray(np.triu(np.ones((tq, tq), np.float32)), dtype=jnp.bfloat16)
    ones = jnp.ones((tq, tq), jnp.bfloat16)
    kern = functools.partial(_dsa_kernel, n_sel=n_sel, tq=tq)
    per_batch = lambda b, j: (b, 0)
    const2 = lambda b, j: (0, 0)
    return pl.pallas_call(
        kern,
        grid=(bsz, nq),
        in_specs=[pl.BlockSpec((N_HEADS, tq, HEAD_DIM), lambda b, j: (0, b * nq + j, 0)),
                  pl.BlockSpec((N_IDX_HEADS, tq, IDX_DIM), lambda b, j: (0, b * nq + j, 0)),
                  pl.BlockSpec((tq, N_IDX_HEADS), lambda b, j: (b * nq + j, 0)),
                  pl.BlockSpec((seq, HEAD_DIM), per_batch),
                  pl.BlockSpec((seq, 2 * HEAD_DIM), per_batch),
                  pl.BlockSpec((seq, IDX_DIM), per_batch),
                  pl.BlockSpec(ut.shape, const2),
                  pl.BlockSpec(ones.shape, const2)],
        out_specs=pl.BlockSpec((tq, ATTN_WIDTH), lambda b, j: (b * nq + j, 0)),
        out_shape=jax.ShapeDtypeStruct((n, ATTN_WIDTH), MXU_DTYPE),
        scratch_shapes=[pltpu.VMEM((nq, tq, tq), jnp.int32), pltpu.VMEM((nq, tq, tq), jnp.float32),
                        pltpu.VMEM((tq, 128), jnp.int32), pltpu.VMEM((tq, 128), jnp.float32)],
        compiler_params=_cparams(("parallel", "parallel")),
        name="dsa",
    )(q, iq, iw, k, v, ik, ut, ones)


def _mix_kernel(x_ref, ada_ref, g1_ref, g2_ref, attn_ref, zp_ref, halo_ref, wg_ref, wgrp_ref, spool_ref,
                wua_ref, wup_ref, wo_ref, wqt_ref, x1_ref, h2t_ref, pqt_ref, *, tiles_per_seq, tm):
    i = pl.program_id(0)
    x = x_ref[...]
    shift1 = ada_ref[0, 0:1, :]
    scale1 = ada_ref[0, 1:2, :]
    gate1 = ada_ref[0, 2:3, :]
    shift2 = ada_ref[0, 3:4, :]
    scale2 = ada_ref[0, 4:5, :]
    h = (_rms_rows(x) * g1_ref[...]) * (1.0 + scale1) + shift1
    zg = jnp.dot(h.astype(MXU_DTYPE), wg_ref[...], preferred_element_type=jnp.float32)
    gates = jax.nn.sigmoid(zg)

    first_in_seq = (i % tiles_per_seq) == 0
    halo = jnp.where(first_in_seq, 0.0, halo_ref[...])
    ext = jnp.concatenate([halo, zp_ref[...]], axis=0)
    t_seq = ((i % tiles_per_seq) * tm + lax.broadcasted_iota(jnp.int32, (tm, 1), 0)).astype(jnp.float32)
    run = ext
    pooled = []
    for g, w in enumerate(POOL_WINDOWS):
        run = run + pltpu.roll(run, w // 2, axis=0)
        cnt = jnp.minimum(t_seq + 1.0, float(w))
        lo, hi = g * POOL_GROUP, (g + 1) * POOL_GROUP
        mean = run[POOL_HALO:, lo:hi] / cnt
        mixed = (mean - ext[POOL_HALO:, lo:hi])
        pooled.append(_mm(mixed, wgrp_ref[g]) * spool_ref[g:g + 1, :])
    pool_out = jnp.concatenate(pooled, axis=1)

    y_attn = jnp.dot(attn_ref[...], wua_ref[...], preferred_element_type=jnp.float32)
    y_pool = _mm(pool_out, wup_ref[...])
    merged = gates[:, 0:D_MODEL] * y_attn + gates[:, D_MODEL:] * y_pool
    x1 = x + gate1 * _mm(merged, wo_ref[...])
    x1_ref[...] = x1

    h2 = (_rms_rows(x1) * g2_ref[...]) * (1.0 + scale2) + shift2
    h2t = h2.T.astype(MXU_DTYPE)
    h2t_ref[...] = h2t
    pqt_ref[...] = jnp.dot(wqt_ref[...], h2t, preferred_element_type=jnp.float32)


def _mix_call(x2, ada3, g1, g2, attn, zp, wg, wgrp, spool, wua, wup, wo, wqt, seq, tm):
    n, d = x2.shape
    tiles_per_seq = seq // tm
    halo_per_tile = tm // POOL_HALO
    kern = functools.partial(_mix_kernel, tiles_per_seq=tiles_per_seq, tm=tm)
    const2 = lambda i: (0, 0)
    const3 = lambda i: (0, 0, 0)
    return pl.pallas_call(
        kern,
        grid=(n // tm,),
        in_specs=[pl.BlockSpec((tm, d), lambda i: (i, 0)),
                  pl.BlockSpec((1, 6, d), lambda i: (i // tiles_per_seq, 0, 0)),
                  pl.BlockSpec((1, d), const2),
                  pl.BlockSpec((1, d), const2),
                  pl.BlockSpec((tm, ATTN_WIDTH), lambda i: (i, 0)),
                  pl.BlockSpec((tm, POOL_WIDTH), lambda i: (i, 0)),
                  pl.BlockSpec((POOL_HALO, POOL_WIDTH), lambda i: (jnp.maximum(i * halo_per_tile - 1, 0), 0)),
                  pl.BlockSpec(wg.shape, const2),
                  pl.BlockSpec(wgrp.shape, const3),
                  pl.BlockSpec(spool.shape, const2),
                  pl.BlockSpec(wua.shape, const2),
                  pl.BlockSpec(wup.shape, const2),
                  pl.BlockSpec(wo.shape, const2),
                  pl.BlockSpec(wqt.shape, const2)],
        out_specs=(pl.BlockSpec((tm, d), lambda i: (i, 0)),
                   pl.BlockSpec((d, tm), lambda i: (0, i)),
                   pl.BlockSpec((d, tm), lambda i: (0, i))),
        out_shape=(jax.ShapeDtypeStruct((n, d), jnp.float32),
                   jax.ShapeDtypeStruct((d, n), MXU_DTYPE),
                   jax.ShapeDtypeStruct((d, n), jnp.float32)),
        compiler_params=_cparams(("parallel",)),
        name="mix",
    )(x2, ada3, g1, g2, attn, zp, zp, wg, wgrp, spool, wua, wup, wo, wqt)


_STAIR = [(r1, r2) for r1 in range(PEER_TOPK) for r2 in range(PEER_TOPK // (r1 + 1))]
_STAIR_ROWS = -(-len(_STAIR) // 8) * 8


def _stair_maps(te):
    r1 = np.full((_STAIR_ROWS, 1), -1.0, np.float32)
    r2 = np.full((_STAIR_ROWS, 1), -1.0, np.float32)
    for row, (a, b) in enumerate(_STAIR):
        r1[row, 0], r2[row, 0] = a, b
    seg = (np.arange(PEER_TOPK, dtype=np.float32)[:, None] == r1[None, :, 0]).astype(np.float32)
    return jnp.asarray(np.tile(r1, (1, te))), jnp.asarray(np.tile(r2, (1, te))), jnp.asarray(seg)


def _extract_top(s, n_rounds, tie_exact):
    r, t = s.shape
    rows = lax.broadcasted_iota(jnp.int32, (r, t), 0).astype(jnp.float32)
    rank = jnp.full((r, t), float(n_rounds), jnp.float32)
    vals = []
    for rnd in range(n_rounds):
        m = jnp.max(s, axis=0, keepdims=True)
        hit = s == m
        if tie_exact:
            first = jnp.min(jnp.where(hit, rows, float(r)), axis=0, keepdims=True)
            hit = rows == first
        rank = jnp.where(hit, float(rnd), rank)
        s = jnp.where(hit, -jnp.inf, s)
        vals.append(m)
    return vals, rank


def _count_rows(mask):
    return jnp.sum(jnp.where(mask, 1.0, 0.0), axis=0, keepdims=True)


def _head_select(sc0, sc1, r1map, r2map, seg, tie_exact):
    kf = float(PEER_TOPK)
    vals0, rk0 = _extract_top(sc0, PEER_TOPK, tie_exact)
    vals1, rk1 = _extract_top(sc1, PEER_TOPK, tie_exact)
    a1 = jnp.full(r1map.shape, -jnp.inf, jnp.float32)
    a2 = jnp.zeros(r2map.shape, jnp.float32)
    for r in range(PEER_TOPK):
        a1 = jnp.where(r1map == float(r), vals0[r], a1)
        a2 = jnp.where(r2map == float(r), vals1[r], a2)
    best, rkc = _extract_top(a1 + a2, PEER_TOPK, tie_exact)
    picked = rkc < kf
    counts = jnp.dot(seg, jnp.where(picked, 1.0, 0.0), preferred_element_type=jnp.float32)
    c_dense = jnp.zeros(sc0.shape, jnp.float32)
    for r in range(PEER_TOPK):
        c_dense = jnp.where(rk0 == float(r), counts[r:r + 1, :], c_dense)
    zsum = jnp.ones_like(best[0])
    for j in range(1, PEER_TOPK):
        zsum = zsum + jnp.exp(best[j] - best[0])
    p1 = jnp.where(rk0 < kf, jnp.exp(sc0 - vals0[0]), 0.0) / zsum
    p2 = jnp.where(rk1 < kf, jnp.exp(sc1 - vals1[0]), 0.0)
    bad = (_count_rows(rk0 < kf) != kf) | (_count_rows(rk1 < kf) != kf) | (_count_rows(picked) != kf)
    return c_dense, p1, rk1, p2, jnp.sum(jnp.where(bad, 1.0, 0.0))


def _peer_select_kernel(pqt_ref, sub_ref, r1map_ref, r2map_ref, seg_ref, c_ref, p1_ref, r2_ref, p2_ref):
    def head_step(hh, carry):
        sc = []
        for half in range(2):
            qrows = pqt_ref[pl.ds(pl.multiple_of((hh * 2 + half) * PEER_HALF, PEER_HALF), PEER_HALF), :]
            sc.append(jnp.dot(sub_ref[hh, half], qrows, precision=_HI, preferred_element_type=jnp.float32))

        def emit(res):
            c_ref[hh] = res[0]
            p1_ref[hh] = res[1]
            r2_ref[hh] = res[2].astype(r2_ref.dtype)
            p2_ref[hh] = res[3].astype(p2_ref.dtype)

        fast = _head_select(sc[0], sc[1], r1map_ref[...], r2map_ref[...], seg_ref[...], tie_exact=False)
        emit(fast)

        @pl.when(fast[4] > 0.0)
        def _():
            emit(_head_select(sc[0], sc[1], r1map_ref[...], r2map_ref[...], seg_ref[...], tie_exact=True))

        return carry

    lax.fori_loop(0, PEER_HEADS, head_step, 0)


def _peer_select_call(pqt, sub, te):
    d, n = pqt.shape
    r1map, r2map, seg = _stair_maps(te)
    f32_shape = jax.ShapeDtypeStruct((PEER_HEADS, N_KEYS, n), jnp.float32)
    b16_shape = jax.ShapeDtypeStruct((PEER_HEADS, N_KEYS, n), MXU_DTYPE)
    ospec = pl.BlockSpec((PEER_HEADS, N_KEYS, te), lambda i: (0, 0, i))
    return pl.pallas_call(
        _peer_select_kernel,
        grid=(n // te,),
        in_specs=[pl.BlockSpec((d, te), lambda i: (0, i)),
                  pl.BlockSpec(sub.shape, lambda i: (0, 0, 0, 0)),
                  pl.BlockSpec(r1map.shape, lambda i: (0, 0)),
                  pl.BlockSpec(r2map.shape, lambda i: (0, 0)),
                  pl.BlockSpec(seg.shape, lambda i: (0, 0))],
        out_specs=(ospec, ospec, ospec, ospec),
        out_shape=(f32_shape, f32_shape, b16_shape, b16_shape),
        compiler_params=_cparams(("parallel",)),
        name="peer_select",
    )(pqt, sub, r1map, r2map, seg)


def _peer_dense_kernel(h2t_ref, u_ref, vt_ref, c_ref, p1_ref, r2_ref, p2_ref, x1_ref, ada_ref,
                       o_ref, acc_ref, g_even_ref, g_odd_ref, w_ref, *, keys_per_step, n_chunks):
    j = pl.program_id(1)
    ec = u_ref.shape[0]
    gdt = g_even_ref.dtype

    @pl.when(j == 0)
    def _():
        acc_ref[...] = jnp.zeros_like(acc_ref)
        g_odd_ref[...] = jnp.zeros_like(g_odd_ref)

    first_key = jnp.minimum(j, n_chunks - 1) * keys_per_step

    def step(g_new_ref, g_prev_ref):
        n_sub = ec // PEER_SUB
        rows_per_sub = acc_ref.shape[0] // n_sub
        for sub in range(n_sub):
            lo = sub * PEER_SUB
            for kk in range(PEER_SUB // N_KEYS):
                i1 = first_key + (lo // N_KEYS + kk)
                w = None
                for hh in range(PEER_HEADS):
                    cb = c_ref[hh, pl.ds(i1, 1), :].astype(gdt)
                    pb = p1_ref[hh, pl.ds(i1, 1), :].astype(gdt)
                    term = jnp.where(r2_ref[hh] < cb, p2_ref[hh], jnp.zeros((), gdt)) * pb
                    w = term if w is None else w + term
                w_ref[lo + kk * N_KEYS:lo + (kk + 1) * N_KEYS, :] = w
            rows = slice(sub * rows_per_sub, (sub + 1) * rows_per_sub)
            acc_ref[rows, :] += jnp.dot(vt_ref[rows, :], g_prev_ref[...], preferred_element_type=jnp.float32)
            a = jnp.dot(u_ref[lo:lo + PEER_SUB, :], h2t_ref[...], preferred_element_type=jnp.float32)
            a = a.astype(gdt)
            act = 0.5 * a * (1.0 + lax.erf(a * (2.0 ** -0.5)))
            g_new_ref[lo:lo + PEER_SUB, :] = w_ref[lo:lo + PEER_SUB, :] * act

    pl.when(j % 2 == 0)(functools.partial(step, g_even_ref, g_odd_ref))
    pl.when(j % 2 == 1)(functools.partial(step, g_odd_ref, g_even_ref))

    @pl.when(j == n_chunks)
    def _():
        gate2 = ada_ref[0, 5:6, :]
        o_ref[...] = x1_ref[...] + gate2 * acc_ref[...].T


def _peer_dense_call(h2t, u, vt, c_d, p1_d, r2_d, p2_d, x1, ada3, seq, tt, ec):
    d, n = h2t.shape
    n_chunks = u.shape[0] // ec
    tiles_per_seq = seq // tt
    kern = functools.partial(_peer_dense_kernel, keys_per_step=ec // N_KEYS, n_chunks=n_chunks)
    sel_spec = pl.BlockSpec((PEER_HEADS, N_KEYS, tt), lambda i, j: (0, 0, i))
    return pl.pallas_call(
        kern,
        grid=(n // tt, n_chunks + 1),
        in_specs=[pl.BlockSpec((d, tt), lambda i, j: (0, i)),
                  pl.BlockSpec((ec, d), lambda i, j: (jnp.minimum(j, n_chunks - 1), 0)),
                  pl.BlockSpec((d, ec), lambda i, j: (0, jnp.maximum(j - 1, 0))),
                  sel_spec, sel_spec, sel_spec, sel_spec,
                  pl.BlockSpec((tt, d), lambda i, j: (i, 0)),
                  pl.BlockSpec((1, 6, d), lambda i, j: (i // tiles_per_seq, 0, 0))],
        out_specs=pl.BlockSpec((tt, d), lambda i, j: (i, 0)),
        out_shape=jax.ShapeDtypeStruct((n, d), jnp.float32),
        scratch_shapes=[pltpu.VMEM((d, tt), jnp.float32), pltpu.VMEM((ec, tt), MXU_DTYPE),
                        pltpu.VMEM((ec, tt), MXU_DTYPE), pltpu.VMEM((ec, tt), MXU_DTYPE)],
        compiler_params=_cparams(("parallel", "arbitrary")),
        name="peer_dense",
    )(h2t, u, vt, c_d, p1_d, r2_d, p2_d, x1, ada3)


def _block_diag_mean(width, group):
    idx = np.arange(width) // group
    return jnp.asarray((idx[:, None] == idx[None, :]).astype(np.float32) / group, dtype=jnp.bfloat16)


def _layer(x2, c, w_ada, b_ada, g_norm1, w_in, g_q, g_k, g_ik, w_pool_grp, s_pool, w_up_attn, w_up_pool,
           w_out, g_norm2, w_peer_q, peer_subkeys, peer_u, peer_v, bsz, seq):
    f32 = jnp.float32
    d = D_MODEL
    tm_b = min(512, seq)
    tq = min(512, seq)
    tm_d = min(256, seq)
    te = 256
    tt = min(512, seq)
    ec = 1024

    ada3 = _ada_call(c, w_ada, b_ada).reshape(bsz, 6, d)

    wa = jnp.pad(w_in[:, :_ZP0], ((0, 0), (0, _SLAB_A - _ZP0))).astype(MXU_DTYPE)
    wp = w_in[:, _ZP0:_ZG0].astype(MXU_DTYPE)
    wg = w_in[:, _ZG0:].astype(MXU_DTYPE)
    gcol = jnp.concatenate([jnp.tile(g_q, N_HEADS) * (HEAD_DIM ** -0.5), g_k, jnp.ones((_IK0 - _V0,), f32),
                            g_ik, jnp.ones((_SLAB_A - _IK0 - IDX_DIM,), f32)]).reshape(1, _SLAB_A)
    bd_a = _block_diag_mean(_IQ0, HEAD_DIM)
    bd_b = _block_diag_mean(_SLAB_A - _IK0, IDX_DIM)

    q, k, v, iq, ik, iw, zp = _inproj_call(x2, ada3, g_norm1.reshape(1, d), wa, wp, bd_a, bd_b, gcol, seq, tm_b)

    attn = _dsa_call(q, iq, iw, k, v, ik, bsz, seq, tq)

    x1, h2t, pqt = _mix_call(x2, ada3, g_norm1.reshape(1, d), g_norm2.reshape(1, d), attn, zp, wg,
                             w_pool_grp.astype(MXU_DTYPE), s_pool, w_up_attn.astype(MXU_DTYPE),
                             w_up_pool.astype(MXU_DTYPE), w_out.astype(MXU_DTYPE),
                             w_peer_q.T.astype(MXU_DTYPE), seq, tm_d)

    c_d, p1_d, r2_d, p2_d = _peer_select_call(pqt, peer_subkeys, te)

    return _peer_dense_call(h2t, peer_u.astype(MXU_DTYPE), peer_v.T.astype(MXU_DTYPE),
                            c_d, p1_d, r2_d, p2_d, x1, ada3, seq, tt, ec)


def kernel(x, c, w_ada, b_ada, g_norm1, w_in, g_q, g_k, g_ik, w_pool_grp, s_pool, w_up_attn, w_up_pool, w_out,
           g_norm2, w_peer_q, peer_subkeys, peer_u, peer_v):
    bsz, seq, d = x.shape
    x2 = x.reshape(bsz * seq, d)
    for layer in range(w_ada.shape[0]):
        x2 = _layer(x2, c, w_ada[layer], b_ada[layer], g_norm1[layer], w_in[layer], g_q[layer], g_k[layer],
                    g_ik[layer], w_pool_grp[layer], s_pool[layer], w_up_attn[layer], w_up_pool[layer],
                    w_out[layer], g_norm2[layer], w_peer_q[layer], peer_subkeys[layer], peer_u[layer],
                    peer_v[layer], bsz, seq)
    return x2.reshape(bsz, seq, d)
```

```python
import functools

import jax
import jax.numpy as jnp
import numpy as np
from jax import lax
from jax.experimental import pallas as pl
from jax.experimental.pallas import tpu as pltpu

D_MODEL = 1024
CHUNK = 64
N_HEADS = 8
HEAD_DIM = 64
ATTN_WIDTH = N_HEADS * HEAD_DIM
N_IDX_HEADS = 4
IDX_DIM = 64
TOPK_MAX = 256
DSA_SUB = 128
DSA_STRIP = 64
DSA_PARTS = 2
POOL_WINDOWS = (2, 4, 8, 16)
N_POOL_GROUPS = 4
POOL_WIDTH = 512
POOL_GROUP = POOL_WIDTH // N_POOL_GROUPS
PEER_HEADS = 8
PEER_HALF = 64
N_KEYS = 128
PEER_TOPK = 16
PEER_HEAD_GROUP = 2
PEER_SUB = 256
EPS = 1e-6
POOL_HALO = 16

_Q0, _K0, _V0, _IQ0, _IK0, _IW0 = 0, 512, 576, 640, 896, 960
_SLAB_A = 1024
_ZP0 = 964
_ZG0 = 1476

MXU_DTYPE = jnp.bfloat16
VMEM_LIMIT = 56 * 1024 * 1024

_INT_MIN = -(2 ** 31)
_SOFTMAX_FLOOR = -1e30
_HI = lax.Precision.HIGHEST


def _cparams(sem, flags=None):
    return pltpu.CompilerParams(dimension_semantics=sem, vmem_limit_bytes=VMEM_LIMIT, flags=flags)


def _mm(a, b):
    return jnp.dot(a.astype(MXU_DTYPE), b.astype(MXU_DTYPE), preferred_element_type=jnp.float32)


def _mm_nt(a, b):
    return lax.dot_general(a.astype(MXU_DTYPE), b.astype(MXU_DTYPE), (((1,), (1,)), ((), ())),
                           preferred_element_type=jnp.float32)


def _rms_rows(x):
    return x * lax.rsqrt(jnp.mean(x * x, axis=-1, keepdims=True) + EPS)


def _group_mean_sq(z, bd):
    sq = z * z
    hi = sq.astype(jnp.bfloat16)
    lo = (sq - hi.astype(jnp.float32)).astype(jnp.bfloat16)
    return (jnp.dot(hi, bd, preferred_element_type=jnp.float32)
            + jnp.dot(lo, bd, preferred_element_type=jnp.float32))


def _ada_kernel(c_ref, w_ref, b_ref, o_ref):
    o_ref[...] = jnp.dot(c_ref[...], w_ref[...], precision=_HI,
                         preferred_element_type=jnp.float32) + b_ref[...]


def _ada_call(c, w, b):
    bsz, d = c.shape
    n_out = w.shape[1]
    tn = 1024
    return pl.pallas_call(
        _ada_kernel,
        grid=(n_out // tn,),
        in_specs=[pl.BlockSpec((bsz, d), lambda j: (0, 0)),
                  pl.BlockSpec((d, tn), lambda j: (0, j)),
                  pl.BlockSpec((1, tn), lambda j: (0, j))],
        out_specs=pl.BlockSpec((bsz, tn), lambda j: (0, j)),
        out_shape=jax.ShapeDtypeStruct((bsz, n_out), jnp.float32),
        compiler_params=_cparams(("arbitrary",)),
        name="ada",
    )(c, w, b.reshape(1, n_out))


def _inproj_kernel(x_ref, ada_ref, g1_ref, wa_ref, wp_ref, bd_a_ref, bd_b_ref, gcol_ref,
                   q_ref, k_ref, v_ref, iq_ref, ik_ref, iw_ref, zp_ref):
    x = x_ref[...]
    shift1 = ada_ref[0, 0:1, :]
    scale1 = ada_ref[0, 1:2, :]
    h = (_rms_rows(x) * g1_ref[...]) * (1.0 + scale1) + shift1
    hb = h.astype(MXU_DTYPE)
    z = jnp.dot(hb, wa_ref[...], preferred_element_type=jnp.float32)
    zp_ref[...] = jnp.dot(hb, wp_ref[...], preferred_element_type=jnp.float32)

    za = z[:, 0:_IQ0]
    na = za * lax.rsqrt(_group_mean_sq(za, bd_a_ref[...]) + EPS) * gcol_ref[:, 0:_IQ0]
    for hh in range(N_HEADS):
        q_ref[hh] = na[:, hh * HEAD_DIM:(hh + 1) * HEAD_DIM].astype(q_ref.dtype)
    k_ref[...] = na[:, _K0:_K0 + HEAD_DIM].astype(k_ref.dtype)
    lane = lax.broadcasted_iota(jnp.int32, (z.shape[0], 2 * HEAD_DIM), 1)
    v_ref[...] = jnp.where(lane >= HEAD_DIM, z[:, _K0:_K0 + 2 * HEAD_DIM],
                           jnp.where(lane == 0, 1.0, 0.0)).astype(v_ref.dtype)
    for hh in range(N_IDX_HEADS):
        iq_ref[hh] = (z[:, _IQ0 + hh * IDX_DIM:_IQ0 + (hh + 1) * IDX_DIM] * (IDX_DIM ** -0.5)).astype(iq_ref.dtype)
    zb = z[:, _IK0:_SLAB_A]
    nb = zb * lax.rsqrt(_group_mean_sq(zb, bd_b_ref[...]) + EPS) * gcol_ref[:, _IK0:_SLAB_A]
    ik_ref[...] = nb[:, 0:IDX_DIM].astype(ik_ref.dtype)
    iw_ref[...] = zb[:, IDX_DIM:IDX_DIM + N_IDX_HEADS] * (N_IDX_HEADS ** -0.5)


def _inproj_call(x2, ada3, g1, wa, wp, bd_a, bd_b, gcol, seq, tm):
    n, d = x2.shape
    tiles_per_seq = seq // tm
    f32 = jnp.float32
    out_shape = (
        jax.ShapeDtypeStruct((N_HEADS, n, HEAD_DIM), MXU_DTYPE),
        jax.ShapeDtypeStruct((n, HEAD_DIM), MXU_DTYPE),
        jax.ShapeDtypeStruct((n, 2 * HEAD_DIM), MXU_DTYPE),
        jax.ShapeDtypeStruct((N_IDX_HEADS, n, IDX_DIM), MXU_DTYPE),
        jax.ShapeDtypeStruct((n, IDX_DIM), MXU_DTYPE),
        jax.ShapeDtypeStruct((n, N_IDX_HEADS), f32),
        jax.ShapeDtypeStruct((n, POOL_WIDTH), f32),
    )
    const2 = lambda i: (0, 0)
    return pl.pallas_call(
        _inproj_kernel,
        grid=(n // tm,),
        in_specs=[pl.BlockSpec((tm, d), lambda i: (i, 0)),
                  pl.BlockSpec((1, 6, d), lambda i: (i // tiles_per_seq, 0, 0)),
                  pl.BlockSpec((1, d), const2),
                  pl.BlockSpec(wa.shape, const2),
                  pl.BlockSpec(wp.shape, const2),
                  pl.BlockSpec(bd_a.shape, const2),
                  pl.BlockSpec(bd_b.shape, const2),
                  pl.BlockSpec(gcol.shape, const2)],
        out_specs=(pl.BlockSpec((N_HEADS, tm, HEAD_DIM), lambda i: (0, i, 0)),
                   pl.BlockSpec((tm, HEAD_DIM), lambda i: (i, 0)),
                   pl.BlockSpec((tm, 2 * HEAD_DIM), lambda i: (i, 0)),
                   pl.BlockSpec((N_IDX_HEADS, tm, IDX_DIM), lambda i: (0, i, 0)),
                   pl.BlockSpec((tm, IDX_DIM), lambda i: (i, 0)),
                   pl.BlockSpec((tm, N_IDX_HEADS), lambda i: (i, 0)),
                   pl.BlockSpec((tm, POOL_WIDTH), lambda i: (i, 0))),
        out_shape=out_shape,
        compiler_params=_cparams(("parallel",)),
        name="inproj",
    )(x2, ada3, g1, wa, wp, bd_a, bd_b, gcol)


def _sortable_key(score):
    bits = lax.bitcast_convert_type(score + 0.0, jnp.int32)
    return bits ^ (lax.shift_right_arithmetic(bits, 31) & 0x7FFFFFFF)


def _dsa_kernel(q_ref, iq_ref, iw_ref, k_ref, v_ref, ik_ref, ut_ref, ones_ref, o_ref, key_ref, bias_ref,
                cand_ref, part_ref, *, n_sel, tq):
    blk = pl.program_id(1)
    n_groups = blk + 1
    grp = key_ref.shape[2]
    kf = float(n_sel)
    lanes = grp // 128
    q_chunk = (blk * tq + lax.broadcasted_iota(jnp.int32, (tq, grp), 0)) // CHUNK
    col = lax.broadcasted_iota(jnp.int32, (tq, grp), 1)
    iw = iw_ref[...]

    def build_keys(g, carry):
        ikg = ik_ref[pl.ds(pl.multiple_of(g * grp, grp), grp), :]
        score = None
        for hh in range(N_IDX_HEADS):
            term = jnp.maximum(_mm_nt(iq_ref[hh], ikg), 0.0) * iw[:, hh:hh + 1]
            score = term if score is None else score + term
        key_ref[g] = jnp.where((g * grp + col) // CHUNK <= q_chunk, _sortable_key(score), _INT_MIN)
        return carry

    lax.fori_loop(0, n_groups, build_keys, 0)

    def find_threshold(n):
        part_rows = tq // DSA_PARTS

        def count(part, cand, strictly_greater):
            sums = []
            for st in range(part_rows // DSA_STRIP):
                r0 = part * part_rows + st * DSA_STRIP
                c = jnp.broadcast_to(cand[st * DSA_STRIP:(st + 1) * DSA_STRIP], (DSA_STRIP, 128))
                acc = jnp.zeros((DSA_STRIP, 128), jnp.float32)
                for g in range(n):
                    for t in range(lanes):
                        key = key_ref[g, r0:r0 + DSA_STRIP, t * 128:(t + 1) * 128]
                        acc = acc + jnp.where(key > c if strictly_greater else key >= c, 1.0, 0.0)
                sums.append(acc)
            return jnp.sum(jnp.concatenate(sums, axis=0), axis=1, keepdims=True)

        zero = jnp.zeros((part_rows, 1), jnp.int32)
        start = tuple(jnp.where(count(part, zero, False) >= kf, 0, _INT_MIN).astype(jnp.int32)
                      for part in range(DSA_PARTS))

        def bit_step(i, thrs):
            bit = lax.shift_left(jnp.int32(1), 30 - i)
            return tuple(jnp.where(count(part, thrs[part] | bit, False) >= kf, thrs[part] | bit, thrs[part])
                         for part in range(DSA_PARTS))

        thrs = lax.fori_loop(0, 31, bit_step, start)
        for part in range(DSA_PARTS):
            rows = slice(part * part_rows, (part + 1) * part_rows)
            cand_ref[rows, :] = jnp.broadcast_to(thrs[part], (part_rows, 128))
            part_ref[rows, :] = jnp.broadcast_to(kf - count(part, thrs[part], True), (part_rows, 128))

    for n in range(1, key_ref.shape[0] + 1):
        pl.when(n_groups == n)(functools.partial(find_threshold, n))

    def emit_bias(g, seen):
        thr_t = cand_ref[...]
        room = part_ref[...]
        keys = [key_ref[g, :, t * 128:(t + 1) * 128] for t in range(lanes)]
        eqs = [key == thr_t for key in keys]
        eqb = jnp.concatenate([jnp.where(eq, 1.0, 0.0) for eq in eqs], axis=1).astype(jnp.bfloat16)
        prefix = seen + jnp.dot(eqb, ut_ref[...], preferred_element_type=jnp.float32)
        for t in range(lanes):
            tie_ok = eqs[t] & (prefix[:, t * 128:(t + 1) * 128] <= room)
            sel = ((keys[t] > thr_t) | tie_ok) & (keys[t] != _INT_MIN)
            bias_ref[g, :, t * 128:(t + 1) * 128] = jnp.where(sel, 0.0, -jnp.inf)
        return seen + jnp.dot(eqb, ones_ref[...], preferred_element_type=jnp.float32)

    lax.fori_loop(0, n_groups, emit_bias, jnp.zeros((tq, grp), jnp.float32))

    def sub_block(sub, carry):
        r0 = pl.multiple_of(sub * DSA_SUB, DSA_SUB)
        qs = q_ref[:, pl.ds(r0, DSA_SUB), :].reshape(N_HEADS * DSA_SUB, HEAD_DIM)

        def group_step(g, state):
            m, acc = state
            k0 = pl.multiple_of(g * grp, grp)
            s = _mm_nt(qs, k_ref[pl.ds(k0, grp), :]).reshape(N_HEADS, DSA_SUB, grp)
            s = s + bias_ref[g, pl.ds(r0, DSA_SUB), :][None]
            m_new = jnp.maximum(m, jnp.max(s, axis=2, keepdims=True))
            alpha = jnp.exp(m - m_new)
            p = jnp.exp((s - m_new).astype(MXU_DTYPE))
            pv = jnp.dot(p.reshape(N_HEADS * DSA_SUB, grp), v_ref[pl.ds(k0, grp), :],
                         preferred_element_type=jnp.float32)
            return m_new, alpha * acc + pv.reshape(N_HEADS, DSA_SUB, 2 * HEAD_DIM)

        init = (jnp.full((N_HEADS, DSA_SUB, 1), _SOFTMAX_FLOOR, jnp.float32),
                jnp.zeros((N_HEADS, DSA_SUB, 2 * HEAD_DIM), jnp.float32))
        _, acc = lax.fori_loop(0, n_groups, group_step, init)
        out = acc[:, :, HEAD_DIM:] / acc[:, :, 0:1]
        o_ref[pl.ds(r0, DSA_SUB), :] = jnp.concatenate([out[hh] for hh in range(N_HEADS)], axis=1).astype(o_ref.dtype)
        return carry

    lax.fori_loop(0, tq // DSA_SUB, sub_block, 0)


def _dsa_call(q, iq, iw, k, v, ik, bsz, seq, tq):
    n = k.shape[0]
    n_sel = min(TOPK_MAX, seq // 4)
    nq = seq // tq
    ut = jnp.asarray(np.triu(np.ones((tq, tq), np.float32)), dtype=jnp.bfloat16)
    ones = jnp.ones((tq, tq), jnp.bfloat16)
    kern = functools.partial(_dsa_kernel, n_sel=n_sel, tq=tq)
    per_batch = lambda b, j: (b, 0)
    const2 = lambda b, j: (0, 0)
    return pl.pallas_call(
        kern,
        grid=(bsz, nq),
        in_specs=[pl.BlockSpec((N_HEADS, tq, HEAD_DIM), lambda b, j: (0, b * nq + j, 0)),
                  pl.BlockSpec((N_IDX_HEADS, tq, IDX_DIM), lambda b, j: (0, b * nq + j, 0)),
                  pl.BlockSpec((tq, N_IDX_HEADS), lambda b, j: (b * nq + j, 0)),
                  pl.BlockSpec((seq, HEAD_DIM), per_batch),
                  pl.BlockSpec((seq, 2 * HEAD_DIM), per_batch),
                  pl.BlockSpec((seq, IDX_DIM), per_batch),
                  pl.BlockSpec(ut.shape, const2),
                  pl.BlockSpec(ones.shape, const2)],
        out_specs=pl.BlockSpec((tq, ATTN_WIDTH), lambda b, j: (b * nq + j, 0)),
        out_shape=jax.ShapeDtypeStruct((n, ATTN_WIDTH), MXU_DTYPE),
        scratch_shapes=[pltpu.VMEM((nq, tq, tq), jnp.int32), pltpu.VMEM((nq, tq, tq), jnp.float32),
                        pltpu.VMEM((tq, 128), jnp.int32), pltpu.VMEM((tq, 128), jnp.float32)],
        compiler_params=_cparams(("parallel", "parallel")),
        name="dsa",
    )(q, iq, iw, k, v, ik, ut, ones)


def _mix_kernel(x_ref, ada_ref, g1_ref, g2_ref, attn_ref, zp_ref, halo_ref, wg_ref, wgrp_ref, spool_ref,
                wua_ref, wup_ref, wo_ref, wqt_ref, x1_ref, h2t_ref, pqt_ref, *, tiles_per_seq, tm):
    i = pl.program_id(0)
    x = x_ref[...]
    shift1 = ada_ref[0, 0:1, :]
    scale1 = ada_ref[0, 1:2, :]
    gate1 = ada_ref[0, 2:3, :]
    shift2 = ada_ref[0, 3:4, :]
    scale2 = ada_ref[0, 4:5, :]
    h = (_rms_rows(x) * g1_ref[...]) * (1.0 + scale1) + shift1
    zg = jnp.dot(h.astype(MXU_DTYPE), wg_ref[...], preferred_element_type=jnp.float32)
    gates = jax.nn.sigmoid(zg)

    first_in_seq = (i % tiles_per_seq) == 0
    halo = jnp.where(first_in_seq, 0.0, halo_ref[...])
    ext = jnp.concatenate([halo, zp_ref[...]], axis=0)
    t_seq = ((i % tiles_per_seq) * tm + lax.broadcasted_iota(jnp.int32, (tm, 1), 0)).astype(jnp.float32)
    run = ext
    pooled = []
    for g, w in enumerate(POOL_WINDOWS):
        run = run + pltpu.roll(run, w // 2, axis=0)
        cnt = jnp.minimum(t_seq + 1.0, float(w))
        lo, hi = g * POOL_GROUP, (g + 1) * POOL_GROUP
        mean = run[POOL_HALO:, lo:hi] / cnt
        mixed = (mean - ext[POOL_HALO:, lo:hi])
        pooled.append(_mm(mixed, wgrp_ref[g]) * spool_ref[g:g + 1, :])
    pool_out = jnp.concatenate(pooled, axis=1)

    y_attn = jnp.dot(attn_ref[...], wua_ref[...], preferred_element_type=jnp.float32)
    y_pool = _mm(pool_out, wup_ref[...])
    merged = gates[:, 0:D_MODEL] * y_attn + gates[:, D_MODEL:] * y_pool
    x1 = x + gate1 * _mm(merged, wo_ref[...])
    x1_ref[...] = x1

    h2 = (_rms_rows(x1) * g2_ref[...]) * (1.0 + scale2) + shift2
    h2t = h2.T.astype(MXU_DTYPE)
    h2t_ref[...] = h2t
    pqt_ref[...] = jnp.dot(wqt_ref[...], h2t, preferred_element_type=jnp.float32)


def _mix_call(x2, ada3, g1, g2, attn, zp, wg, wgrp, spool, wua, wup, wo, wqt, seq, tm):
    n, d = x2.shape
    tiles_per_seq = seq // tm
    halo_per_tile = tm // POOL_HALO
    kern = functools.partial(_mix_kernel, tiles_per_seq=tiles_per_seq, tm=tm)
    const2 = lambda i: (0, 0)
    const3 = lambda i: (0, 0, 0)
    return pl.pallas_call(
        kern,
        grid=(n // tm,),
        in_specs=[pl.BlockSpec((tm, d), lambda i: (i, 0)),
                  pl.BlockSpec((1, 6, d), lambda i: (i // tiles_per_seq, 0, 0)),
                  pl.BlockSpec((1, d), const2),
                  pl.BlockSpec((1, d), const2),
                  pl.BlockSpec((tm, ATTN_WIDTH), lambda i: (i, 0)),
                  pl.BlockSpec((tm, POOL_WIDTH), lambda i: (i, 0)),
                  pl.BlockSpec((POOL_HALO, POOL_WIDTH), lambda i: (jnp.maximum(i * halo_per_tile - 1, 0), 0)),
                  pl.BlockSpec(wg.shape, const2),
                  pl.BlockSpec(wgrp.shape, const3),
                  pl.BlockSpec(spool.shape, const2),
                  pl.BlockSpec(wua.shape, const2),
                  pl.BlockSpec(wup.shape, const2),
                  pl.BlockSpec(wo.shape, const2),
                  pl.BlockSpec(wqt.shape, const2)],
        out_specs=(pl.BlockSpec((tm, d), lambda i: (i, 0)),
                   pl.BlockSpec((d, tm), lambda i: (0, i)),
                   pl.BlockSpec((d, tm), lambda i: (0, i))),
        out_shape=(jax.ShapeDtypeStruct((n, d), jnp.float32),
                   jax.ShapeDtypeStruct((d, n), MXU_DTYPE),
                   jax.ShapeDtypeStruct((d, n), jnp.float32)),
        compiler_params=_cparams(("parallel",)),
        name="mix",
    )(x2, ada3, g1, g2, attn, zp, zp, wg, wgrp, spool, wua, wup, wo, wqt)


_STAIR = [(r1, r2) for r1 in range(PEER_TOPK) for r2 in range(PEER_TOPK // (r1 + 1))]
_STAIR_ROWS = -(-len(_STAIR) // 8) * 8


def _stair_maps(te):
    r1 = np.full((_STAIR_ROWS, 1), -1.0, np.float32)
    r2 = np.full((_STAIR_ROWS, 1), -1.0, np.float32)
    for row, (a, b) in enumerate(_STAIR):
        r1[row, 0], r2[row, 0] = a, b
    seg = (np.arange(PEER_TOPK, dtype=np.float32)[:, None] == r1[None, :, 0]).astype(np.float32)
    return jnp.asarray(np.tile(r1, (1, te))), jnp.asarray(np.tile(r2, (1, te))), jnp.asarray(seg)


def _extract_top(s, n_rounds, tie_exact):
    r, t = s.shape
    rows = lax.broadcasted_iota(jnp.int32, (r, t), 0).astype(jnp.float32)
    rank = jnp.full((r, t), float(n_rounds), jnp.float32)
    vals = []
    for rnd in range(n_rounds):
        m = jnp.max(s, axis=0, keepdims=True)
        hit = s == m
        if tie_exact:
            first = jnp.min(jnp.where(hit, rows, float(r)), axis=0, keepdims=True)
            hit = rows == first
        rank = jnp.where(hit, float(rnd), rank)
        s = jnp.where(hit, -jnp.inf, s)
        vals.append(m)
    return vals, rank


def _count_rows(mask):
    return jnp.sum(jnp.where(mask, 1.0, 0.0), axis=0, keepdims=True)


def _head_select(sc0, sc1, r1map, r2map, seg, tie_exact):
    kf = float(PEER_TOPK)
    vals0, rk0 = _extract_top(sc0, PEER_TOPK, tie_exact)
    vals1, rk1 = _extract_top(sc1, PEER_TOPK, tie_exact)
    a1 = jnp.full(r1map.shape, -jnp.inf, jnp.float32)
    a2 = jnp.zeros(r2map.shape, jnp.float32)
    for r in range(PEER_TOPK):
        a1 = jnp.where(r1map == float(r), vals0[r], a1)
        a2 = jnp.where(r2map == float(r), vals1[r], a2)
    best, rkc = _extract_top(a1 + a2, PEER_TOPK, tie_exact)
    picked = rkc < kf
    counts = jnp.dot(seg, jnp.where(picked, 1.0, 0.0), preferred_element_type=jnp.float32)
    c_dense = jnp.zeros(sc0.shape, jnp.float32)
    for r in range(PEER_TOPK):
        c_dense = jnp.where(rk0 == float(r), counts[r:r + 1, :], c_dense)
    zsum = jnp.ones_like(best[0])
    for j in range(1, PEER_TOPK):
        zsum = zsum + jnp.exp(best[j] - best[0])
    p1 = jnp.where(rk0 < kf, jnp.exp(sc0 - vals0[0]), 0.0) / zsum
    p2 = jnp.where(rk1 < kf, jnp.exp(sc1 - vals1[0]), 0.0)
    bad = (_count_rows(rk0 < kf) != kf) | (_count_rows(rk1 < kf) != kf) | (_count_rows(picked) != kf)
    return c_dense, p1, rk1, p2, jnp.sum(jnp.where(bad, 1.0, 0.0))


def _peer_select_kernel(pqt_ref, sub_ref, r1map_ref, r2map_ref, seg_ref, c_ref, p1_ref, r2_ref, p2_ref):
    def scores(hh):
        sc = []
        for half in range(2):
            qrows = pqt_ref[pl.ds(pl.multiple_of((hh * 2 + half) * PEER_HALF, PEER_HALF), PEER_HALF), :]
            sc.append(jnp.dot(sub_ref[hh, half], qrows, precision=_HI, preferred_element_type=jnp.float32))
        return sc

    def emit(hh, res):
        c_ref[hh] = res[0]
        p1_ref[hh] = res[1]
        r2_ref[hh] = res[2].astype(r2_ref.dtype)
        p2_ref[hh] = res[3].astype(p2_ref.dtype)

    def head_group_step(i, carry):
        heads = [i * PEER_HEAD_GROUP + k for k in range(PEER_HEAD_GROUP)]
        sc = [scores(hh) for hh in heads]
        fast = [_head_select(s[0], s[1], r1map_ref[...], r2map_ref[...], seg_ref[...], tie_exact=False) for s in sc]
        for hh, res in zip(heads, fast):
            emit(hh, res)
        for hh, s, res in zip(heads, sc, fast):
            @pl.when(res[4] > 0.0)
            def _(hh=hh, s=s):
                emit(hh, _head_select(s[0], s[1], r1map_ref[...], r2map_ref[...], seg_ref[...], tie_exact=True))
        return carry

    lax.fori_loop(0, PEER_HEADS // PEER_HEAD_GROUP, head_group_step, 0)


def _peer_select_call(pqt, sub, te):
    d, n = pqt.shape
    r1map, r2map, seg = _stair_maps(te)
    f32_shape = jax.ShapeDtypeStruct((PEER_HEADS, N_KEYS, n), jnp.float32)
    b16_shape = jax.ShapeDtypeStruct((PEER_HEADS, N_KEYS, n), MXU_DTYPE)
    ospec = pl.BlockSpec((PEER_HEADS, N_KEYS, te), lambda i: (0, 0, i))
    return pl.pallas_call(
        _peer_select_kernel,
        grid=(n // te,),
        in_specs=[pl.BlockSpec((d, te), lambda i: (0, i)),
                  pl.BlockSpec(sub.shape, lambda i: (0, 0, 0, 0)),
                  pl.BlockSpec(r1map.shape, lambda i: (0, 0)),
                  pl.BlockSpec(r2map.shape, lambda i: (0, 0)),
                  pl.BlockSpec(seg.shape, lambda i: (0, 0))],
        out_specs=(ospec, ospec, ospec, ospec),
        out_shape=(f32_shape, f32_shape, b16_shape, b16_shape),
        compiler_params=_cparams(("parallel",)),
        name="peer_select",
    )(pqt, sub, r1map, r2map, seg)


def _peer_dense_kernel(h2t_ref, u_ref, vt_ref, c_ref, p1_ref, r2_ref, p2_ref, x1_ref, ada_ref,
                       o_ref, acc_ref, g_even_ref, g_odd_ref, w_ref, *, keys_per_step, n_chunks):
    j = pl.program_id(1)
    ec = u_ref.shape[0]
    gdt = g_even_ref.dtype

    @pl.when(j == 0)
    def _():
        acc_ref[...] = jnp.zeros_like(acc_ref)
        g_odd_ref[...] = jnp.zeros_like(g_odd_ref)

    first_key = jnp.minimum(j, n_chunks - 1) * keys_per_step

    def step(g_new_ref, g_prev_ref):
        n_sub = ec // PEER_SUB
        rows_per_sub = acc_ref.shape[0] // n_sub
        for sub in range(n_sub):
            lo = sub * PEER_SUB
            for kk in range(PEER_SUB // N_KEYS):
                i1 = first_key + (lo // N_KEYS + kk)
                w = None
                for hh in range(PEER_HEADS):
                    cb = c_ref[hh, pl.ds(i1, 1), :].astype(gdt)
                    pb = p1_ref[hh, pl.ds(i1, 1), :].astype(gdt)
                    term = jnp.where(r2_ref[hh] < cb, p2_ref[hh], jnp.zeros((), gdt)) * pb
                    w = term if w is None else w + term
                w_ref[lo + kk * N_KEYS:lo + (kk + 1) * N_KEYS, :] = w
            rows = slice(sub * rows_per_sub, (sub + 1) * rows_per_sub)
            acc_ref[rows, :] += jnp.dot(vt_ref[rows, :], g_prev_ref[...], preferred_element_type=jnp.float32)
            a = jnp.dot(u_ref[lo:lo + PEER_SUB, :], h2t_ref[...], preferred_element_type=jnp.float32)
            a = a.astype(gdt)
            act = 0.5 * a * (1.0 + lax.erf(a * (2.0 ** -0.5)))
            g_new_ref[lo:lo + PEER_SUB, :] = w_ref[lo:lo + PEER_SUB, :] * act

    pl.when((j % 2 == 0) & (j < n_chunks))(functools.partial(step, g_even_ref, g_odd_ref))
    pl.when((j % 2 == 1) & (j < n_chunks))(functools.partial(step, g_odd_ref, g_even_ref))

    @pl.when(j == n_chunks)
    def _():
        g_last_ref = g_odd_ref if n_chunks % 2 == 0 else g_even_ref
        acc = acc_ref[...] + jnp.dot(vt_ref[...], g_last_ref[...], preferred_element_type=jnp.float32)
        gate2 = ada_ref[0, 5:6, :]
        o_ref[...] = x1_ref[...] + gate2 * acc.T


def _peer_dense_call(h2t, u, vt, c_d, p1_d, r2_d, p2_d, x1, ada3, seq, tt, ec):
    d, n = h2t.shape
    n_chunks = u.shape[0] // ec
    tiles_per_seq = seq // tt
    kern = functools.partial(_peer_dense_kernel, keys_per_step=ec // N_KEYS, n_chunks=n_chunks)
    sel_spec = pl.BlockSpec((PEER_HEADS, N_KEYS, tt), lambda i, j: (0, 0, i))
    return pl.pallas_call(
        kern,
        grid=(n // tt, n_chunks + 1),
        in_specs=[pl.BlockSpec((d, tt), lambda i, j: (0, i)),
                  pl.BlockSpec((ec, d), lambda i, j: (jnp.minimum(j, n_chunks - 1), 0)),
                  pl.BlockSpec((d, ec), lambda i, j: (0, jnp.maximum(j - 1, 0))),
                  sel_spec, sel_spec, sel_spec, sel_spec,
                  pl.BlockSpec((tt, d), lambda i, j: (i, 0)),
                  pl.BlockSpec((1, 6, d), lambda i, j: (i // tiles_per_seq, 0, 0))],
        out_specs=pl.BlockSpec((tt, d), lambda i, j: (i, 0)),
        out_shape=jax.ShapeDtypeStruct((n, d), jnp.float32),
        scratch_shapes=[pltpu.VMEM((d, tt), jnp.float32), pltpu.VMEM((ec, tt), MXU_DTYPE),
                        pltpu.VMEM((ec, tt), MXU_DTYPE), pltpu.VMEM((ec, tt), MXU_DTYPE)],
        compiler_params=_cparams(("parallel", "arbitrary")),
        name="peer_dense",
    )(h2t, u, vt, c_d, p1_d, r2_d, p2_d, x1, ada3)


def _block_diag_mean(width, group):
    idx = np.arange(width) // group
    return jnp.asarray((idx[:, None] == idx[None, :]).astype(np.float32) / group, dtype=jnp.bfloat16)


def _layer(x2, c, w_ada, b_ada, g_norm1, w_in, g_q, g_k, g_ik, w_pool_grp, s_pool, w_up_attn, w_up_pool,
           w_out, g_norm2, w_peer_q, peer_subkeys, peer_u, peer_v, bsz, seq):
    f32 = jnp.float32
    d = D_MODEL
    tm_b = min(512, seq)
    tq = min(512, seq)
    tm_d = min(512, seq)
    te = 256
    tt = min(512, seq)
    ec = 1024

    ada3 = _ada_call(c, w_ada, b_ada).reshape(bsz, 6, d)

    wa = jnp.pad(w_in[:, :_ZP0], ((0, 0), (0, _SLAB_A - _ZP0))).astype(MXU_DTYPE)
    wp = w_in[:, _ZP0:_ZG0].astype(MXU_DTYPE)
    wg = w_in[:, _ZG0:].astype(MXU_DTYPE)
    gcol = jnp.concatenate([jnp.tile(g_q, N_HEADS) * (HEAD_DIM ** -0.5), g_k, jnp.ones((_IK0 - _V0,), f32),
                            g_ik, jnp.ones((_SLAB_A - _IK0 - IDX_DIM,), f32)]).reshape(1, _SLAB_A)
    bd_a = _block_diag_mean(_IQ0, HEAD_DIM)
    bd_b = _block_diag_mean(_SLAB_A - _IK0, IDX_DIM)

    q, k, v, iq, ik, iw, zp = _inproj_call(x2, ada3, g_norm1.reshape(1, d), wa, wp, bd_a, bd_b, gcol, seq, tm_b)

    attn = _dsa_call(q, iq, iw, k, v, ik, bsz, seq, tq)

    x1, h2t, pqt = _mix_call(x2, ada3, g_norm1.reshape(1, d), g_norm2.reshape(1, d), attn, zp, wg,
                             w_pool_grp.astype(MXU_DTYPE), s_pool, w_up_attn.astype(MXU_DTYPE),
                             w_up_pool.astype(MXU_DTYPE), w_out.astype(MXU_DTYPE),
                             w_peer_q.T.astype(MXU_DTYPE), seq, tm_d)

    c_d, p1_d, r2_d, p2_d = _peer_select_call(pqt, peer_subkeys, te)

    return _peer_dense_call(h2t, peer_u.astype(MXU_DTYPE), peer_v.T.astype(MXU_DTYPE),
                            c_d, p1_d, r2_d, p2_d, x1, ada3, seq, tt, ec)


def kernel(x, c, w_ada, b_ada, g_norm1, w_in, g_q, g_k, g_ik, w_pool_grp, s_pool, w_up_attn, w_up_pool, w_out,
           g_norm2, w_peer_q, peer_subkeys, peer_u, peer_v):
    bsz, seq, d = x.shape
    x2 = x.reshape(bsz * seq, d)
    for layer in range(w_ada.shape[0]):
        x2 = _layer(x2, c, w_ada[layer], b_ada[layer], g_norm1[layer], w_in[layer], g_q[layer], g_k[layer],
                    g_ik[layer], w_pool_grp[layer], s_pool[layer], w_up_attn[layer], w_up_pool[layer],
                    w_out[layer], g_norm2[layer], w_peer_q[layer], peer_subkeys[layer], peer_u[layer],
                    peer_v[layer], bsz, seq)
    return x2.reshape(bsz, seq, d)
```

```python
import functools

import jax
import jax.numpy as jnp
import numpy as np
from jax import lax
from jax.experimental import pallas as pl
from jax.experimental.pallas import tpu as pltpu

D_MODEL = 1024
CHUNK = 64
N_HEADS = 8
HEAD_DIM = 64
ATTN_WIDTH = N_HEADS * HEAD_DIM
N_IDX_HEADS = 4
IDX_DIM = 64
TOPK_MAX = 256
DSA_SUB = 256
DSA_STRIP = 64
DSA_PARTS = 2
POOL_WINDOWS = (2, 4, 8, 16)
N_POOL_GROUPS = 4
POOL_WIDTH = 512
POOL_GROUP = POOL_WIDTH // N_POOL_GROUPS
PEER_HEADS = 8
PEER_HALF = 64
N_KEYS = 128
PEER_TOPK = 16
PEER_HEAD_GROUP = 2
GATE_ROWS = 16
PEER_SUB = 256
EPS = 1e-6
POOL_HALO = 16

_Q0, _K0, _V0, _IQ0, _IK0, _IW0 = 0, 512, 576, 640, 896, 960
_SLAB_A = 1024
_ZP0 = 964
_ZG0 = 1476

MXU_DTYPE = jnp.bfloat16
VMEM_LIMIT = 56 * 1024 * 1024

_INT_MIN = -(2 ** 31)
_SOFTMAX_FLOOR = -1e30
_HI = lax.Precision.HIGHEST


def _cparams(sem, flags=None):
    return pltpu.CompilerParams(dimension_semantics=sem, vmem_limit_bytes=VMEM_LIMIT, flags=flags)


def _mm(a, b):
    return jnp.dot(a.astype(MXU_DTYPE), b.astype(MXU_DTYPE), preferred_element_type=jnp.float32)


def _mm_nt(a, b):
    return lax.dot_general(a.astype(MXU_DTYPE), b.astype(MXU_DTYPE), (((1,), (1,)), ((), ())),
                           preferred_element_type=jnp.float32)


def _rms_rows(x):
    return x * lax.rsqrt(jnp.mean(x * x, axis=-1, keepdims=True) + EPS)


def _group_mean_sq(z, bd):
    sq = z * z
    hi = sq.astype(jnp.bfloat16)
    lo = (sq - hi.astype(jnp.float32)).astype(jnp.bfloat16)
    return (jnp.dot(hi, bd, preferred_element_type=jnp.float32)
            + jnp.dot(lo, bd, preferred_element_type=jnp.float32))


def _ada_kernel(c_ref, w_ref, b_ref, o_ref):
    o_ref[...] = jnp.dot(c_ref[...], w_ref[...], precision=_HI,
                         preferred_element_type=jnp.float32) + b_ref[...]


def _ada_call(c, w, b):
    bsz, d = c.shape
    n_out = w.shape[1]
    tn = 1024
    return pl.pallas_call(
        _ada_kernel,
        grid=(n_out // tn,),
        in_specs=[pl.BlockSpec((bsz, d), lambda j: (0, 0)),
                  pl.BlockSpec((d, tn), lambda j: (0, j)),
                  pl.BlockSpec((1, tn), lambda j: (0, j))],
        out_specs=pl.BlockSpec((bsz, tn), lambda j: (0, j)),
        out_shape=jax.ShapeDtypeStruct((bsz, n_out), jnp.float32),
        compiler_params=_cparams(("arbitrary",)),
        name="ada",
    )(c, w, b.reshape(1, n_out))


def _inproj_kernel(x_ref, ada_ref, g1_ref, wa_ref, wp_ref, bd_a_ref, bd_b_ref, gcol_ref,
                   q_ref, k_ref, v_ref, iq_ref, ik_ref, iw_ref, zp_ref):
    x = x_ref[...]
    shift1 = ada_ref[0, 0:1, :]
    scale1 = ada_ref[0, 1:2, :]
    h = (_rms_rows(x) * g1_ref[...]) * (1.0 + scale1) + shift1
    hb = h.astype(MXU_DTYPE)
    z = jnp.dot(hb, wa_ref[...], preferred_element_type=jnp.float32)
    zp_ref[...] = jnp.dot(hb, wp_ref[...], preferred_element_type=jnp.float32)

    za = z[:, 0:_IQ0]
    na = za * lax.rsqrt(_group_mean_sq(za, bd_a_ref[...]) + EPS) * gcol_ref[:, 0:_IQ0]
    for hh in range(N_HEADS):
        q_ref[hh] = na[:, hh * HEAD_DIM:(hh + 1) * HEAD_DIM].astype(q_ref.dtype)
    k_ref[...] = na[:, _K0:_K0 + HEAD_DIM].astype(k_ref.dtype)
    lane = lax.broadcasted_iota(jnp.int32, (z.shape[0], 2 * HEAD_DIM), 1)
    v_ref[...] = jnp.where(lane >= HEAD_DIM, z[:, _K0:_K0 + 2 * HEAD_DIM],
                           jnp.where(lane == 0, 1.0, 0.0)).astype(v_ref.dtype)
    for hh in range(N_IDX_HEADS):
        iq_ref[hh] = (z[:, _IQ0 + hh * IDX_DIM:_IQ0 + (hh + 1) * IDX_DIM] * (IDX_DIM ** -0.5)).astype(iq_ref.dtype)
    zb = z[:, _IK0:_SLAB_A]
    nb = zb * lax.rsqrt(_group_mean_sq(zb, bd_b_ref[...]) + EPS) * gcol_ref[:, _IK0:_SLAB_A]
    ik_ref[...] = nb[:, 0:IDX_DIM].astype(ik_ref.dtype)
    iw_ref[...] = zb[:, IDX_DIM:IDX_DIM + N_IDX_HEADS] * (N_IDX_HEADS ** -0.5)


def _inproj_call(x2, ada3, g1, wa, wp, bd_a, bd_b, gcol, seq, tm):
    n, d = x2.shape
    tiles_per_seq = seq // tm
    f32 = jnp.float32
    out_shape = (
        jax.ShapeDtypeStruct((N_HEADS, n, HEAD_DIM), MXU_DTYPE),
        jax.ShapeDtypeStruct((n, HEAD_DIM), MXU_DTYPE),
        jax.ShapeDtypeStruct((n, 2 * HEAD_DIM), MXU_DTYPE),
        jax.ShapeDtypeStruct((N_IDX_HEADS, n, IDX_DIM), MXU_DTYPE),
        jax.ShapeDtypeStruct((n, IDX_DIM), MXU_DTYPE),
        jax.ShapeDtypeStruct((n, N_IDX_HEADS), f32),
        jax.ShapeDtypeStruct((n, POOL_WIDTH), f32),
    )
    const2 = lambda i: (0, 0)
    return pl.pallas_call(
        _inproj_kernel,
        grid=(n // tm,),
        in_specs=[pl.BlockSpec((tm, d), lambda i: (i, 0)),
                  pl.BlockSpec((1, 6, d), lambda i: (i // tiles_per_seq, 0, 0)),
                  pl.BlockSpec((1, d), const2),
                  pl.BlockSpec(wa.shape, const2),
                  pl.BlockSpec(wp.shape, const2),
                  pl.BlockSpec(bd_a.shape, const2),
                  pl.BlockSpec(bd_b.shape, const2),
                  pl.BlockSpec(gcol.shape, const2)],
        out_specs=(pl.BlockSpec((N_HEADS, tm, HEAD_DIM), lambda i: (0, i, 0)),
                   pl.BlockSpec((tm, HEAD_DIM), lambda i: (i, 0)),
                   pl.BlockSpec((tm, 2 * HEAD_DIM), lambda i: (i, 0)),
                   pl.BlockSpec((N_IDX_HEADS, tm, IDX_DIM), lambda i: (0, i, 0)),
                   pl.BlockSpec((tm, IDX_DIM), lambda i: (i, 0)),
                   pl.BlockSpec((tm, N_IDX_HEADS), lambda i: (i, 0)),
                   pl.BlockSpec((tm, POOL_WIDTH), lambda i: (i, 0))),
        out_shape=out_shape,
        compiler_params=_cparams(("parallel",)),
        name="inproj",
    )(x2, ada3, g1, wa, wp, bd_a, bd_b, gcol)


def _sortable_key(score):
    bits = lax.bitcast_convert_type(score + 0.0, jnp.int32)
    return bits ^ (lax.shift_right_arithmetic(bits, 31) & 0x7FFFFFFF)


def _dsa_kernel(q_ref, iq_ref, iw_ref, k_ref, v_ref, ik_ref, ut_ref, ones_ref, o_ref, key_ref, bias_ref,
                cand_ref, part_ref, *, n_sel, tq):
    blk = pl.program_id(1)
    n_groups = blk + 1
    grp = key_ref.shape[2]
    kf = float(n_sel)
    lanes = grp // 128
    q_chunk = (blk * tq + lax.broadcasted_iota(jnp.int32, (tq, grp), 0)) // CHUNK
    col = lax.broadcasted_iota(jnp.int32, (tq, grp), 1)
    iw = iw_ref[...]

    def build_keys(g, carry):
        ikg = ik_ref[pl.ds(pl.multiple_of(g * grp, grp), grp), :]
        score = None
        for hh in range(N_IDX_HEADS):
            term = jnp.maximum(_mm_nt(iq_ref[hh], ikg), 0.0) * iw[:, hh:hh + 1]
            score = term if score is None else score + term
        key_ref[g] = jnp.where((g * grp + col) // CHUNK <= q_chunk, _sortable_key(score), _INT_MIN)
        return carry

    lax.fori_loop(0, n_groups, build_keys, 0)

    def find_threshold(n):
        part_rows = tq // DSA_PARTS

        def count(part, cand, strictly_greater):
            sums = []
            for st in range(part_rows // DSA_STRIP):
                r0 = part * part_rows + st * DSA_STRIP
                c = jnp.broadcast_to(cand[st * DSA_STRIP:(st + 1) * DSA_STRIP], (DSA_STRIP, 128))
                acc = jnp.zeros((DSA_STRIP, 128), jnp.float32)
                for g in range(n):
                    for t in range(lanes):
                        key = key_ref[g, r0:r0 + DSA_STRIP, t * 128:(t + 1) * 128]
                        acc = acc + jnp.where(key > c if strictly_greater else key >= c, 1.0, 0.0)
                sums.append(acc)
            return jnp.sum(jnp.concatenate(sums, axis=0), axis=1, keepdims=True)

        zero = jnp.zeros((part_rows, 1), jnp.int32)
        start = tuple(jnp.where(count(part, zero, False) >= kf, 0, _INT_MIN).astype(jnp.int32)
                      for part in range(DSA_PARTS))

        def bit_step(i, thrs):
            bit = lax.shift_left(jnp.int32(1), 30 - i)
            return tuple(jnp.where(count(part, thrs[part] | bit, False) >= kf, thrs[part] | bit, thrs[part])
                         for part in range(DSA_PARTS))

        thrs = lax.fori_loop(0, 31, bit_step, start)
        for part in range(DSA_PARTS):
            rows = slice(part * part_rows, (part + 1) * part_rows)
            cand_ref[rows, :] = jnp.broadcast_to(thrs[part], (part_rows, 128))
            part_ref[rows, :] = jnp.broadcast_to(kf - count(part, thrs[part], True), (part_rows, 128))

    for n in range(1, key_ref.shape[0] + 1):
        pl.when(n_groups == n)(functools.partial(find_threshold, n))

    def emit_bias(g, seen):
        thr_t = cand_ref[...]
        room = part_ref[...]
        keys = [key_ref[g, :, t * 128:(t + 1) * 128] for t in range(lanes)]
        eqs = [key == thr_t for key in keys]
        eqb = jnp.concatenate([jnp.where(eq, 1.0, 0.0) for eq in eqs], axis=1).astype(jnp.bfloat16)
        prefix = seen + jnp.dot(eqb, ut_ref[...], preferred_element_type=jnp.float32)
        for t in range(lanes):
            tie_ok = eqs[t] & (prefix[:, t * 128:(t + 1) * 128] <= room)
            sel = ((keys[t] > thr_t) | tie_ok) & (keys[t] != _INT_MIN)
            bias_ref[g, :, t * 128:(t + 1) * 128] = jnp.where(sel, 0.0, -jnp.inf)
        return seen + jnp.dot(eqb, ones_ref[...], preferred_element_type=jnp.float32)

    lax.fori_loop(0, n_groups, emit_bias, jnp.zeros((tq, grp), jnp.float32))

    def sub_block(sub, carry):
        r0 = pl.multiple_of(sub * DSA_SUB, DSA_SUB)
        qs = q_ref[:, pl.ds(r0, DSA_SUB), :].reshape(N_HEADS * DSA_SUB, HEAD_DIM)

        def group_step(g, state):
            m, acc = state
            k0 = pl.multiple_of(g * grp, grp)
            s = _mm_nt(qs, k_ref[pl.ds(k0, grp), :]).reshape(N_HEADS, DSA_SUB, grp)
            s = s + bias_ref[g, pl.ds(r0, DSA_SUB), :][None]
            m_new = jnp.maximum(m, jnp.max(s, axis=2, keepdims=True))
            alpha = jnp.exp(m - m_new)
            p = jnp.exp((s - m_new).astype(MXU_DTYPE))
            pv = jnp.dot(p.reshape(N_HEADS * DSA_SUB, grp), v_ref[pl.ds(k0, grp), :],
                         preferred_element_type=jnp.float32)
            return m_new, alpha * acc + pv.reshape(N_HEADS, DSA_SUB, 2 * HEAD_DIM)

        init = (jnp.full((N_HEADS, DSA_SUB, 1), _SOFTMAX_FLOOR, jnp.float32),
                jnp.zeros((N_HEADS, DSA_SUB, 2 * HEAD_DIM), jnp.float32))
        _, acc = lax.fori_loop(0, n_groups, group_step, init)
        out = acc[:, :, HEAD_DIM:] / acc[:, :, 0:1]
        o_ref[pl.ds(r0, DSA_SUB), :] = jnp.concatenate([out[hh] for hh in range(N_HEADS)], axis=1).astype(o_ref.dtype)
        return carry

    lax.fori_loop(0, tq // DSA_SUB, sub_block, 0)


def _dsa_call(q, iq, iw, k, v, ik, bsz, seq, tq):
    n = k.shape[0]
    n_sel = min(TOPK_MAX, seq // 4)
    nq = seq // tq
    ut = jnp.asarray(np.triu(np.ones((tq, tq), np.float32)), dtype=jnp.bfloat16)
    ones = jnp.ones((tq, tq), jnp.bfloat16)
    kern = functools.partial(_dsa_kernel, n_sel=n_sel, tq=tq)
    per_batch = lambda b, j: (b, 0)
    const2 = lambda b, j: (0, 0)
    return pl.pallas_call(
        kern,
        grid=(bsz, nq),
        in_specs=[pl.BlockSpec((N_HEADS, tq, HEAD_DIM), lambda b, j: (0, b * nq + j, 0)),
                  pl.BlockSpec((N_IDX_HEADS, tq, IDX_DIM), lambda b, j: (0, b * nq + j, 0)),
                  pl.BlockSpec((tq, N_IDX_HEADS), lambda b, j: (b * nq + j, 0)),
                  pl.BlockSpec((seq, HEAD_DIM), per_batch),
                  pl.BlockSpec((seq, 2 * HEAD_DIM), per_batch),
                  pl.BlockSpec((seq, IDX_DIM), per_batch),
                  pl.BlockSpec(ut.shape, const2),
                  pl.BlockSpec(ones.shape, const2)],
        out_specs=pl.BlockSpec((tq, ATTN_WIDTH), lambda b, j: (b * nq + j, 0)),
        out_shape=jax.ShapeDtypeStruct((n, ATTN_WIDTH), MXU_DTYPE),
        scratch_shapes=[pltpu.VMEM((nq, tq, tq), jnp.int32), pltpu.VMEM((nq, tq, tq), jnp.float32),
                        pltpu.VMEM((tq, 128), jnp.int32), pltpu.VMEM((tq, 128), jnp.float32)],
        compiler_params=_cparams(("parallel", "parallel")),
        name="dsa",
    )(q, iq, iw, k, v, ik, ut, ones)


def _mix_kernel(x_ref, ada_ref, g1_ref, g2_ref, attn_ref, zp_ref, halo_ref, wg_ref, wgrp_ref, spool_ref,
                wua_ref, wup_ref, wo_ref, wqt_ref, x1_ref, h2t_ref, pqt_ref, *, tiles_per_seq, tm):
    i = pl.program_id(0)
    x = x_ref[...]
    shift1 = ada_ref[0, 0:1, :]
    scale1 = ada_ref[0, 1:2, :]
    gate1 = ada_ref[0, 2:3, :]
    shift2 = ada_ref[0, 3:4, :]
    scale2 = ada_ref[0, 4:5, :]
    h = (_rms_rows(x) * g1_ref[...]) * (1.0 + scale1) + shift1
    zg = jnp.dot(h.astype(MXU_DTYPE), wg_ref[...], preferred_element_type=jnp.float32)
    gates = jax.nn.sigmoid(zg)

    first_in_seq = (i % tiles_per_seq) == 0
    halo = jnp.where(first_in_seq, 0.0, halo_ref[...])
    ext = jnp.concatenate([halo, zp_ref[...]], axis=0)
    t_seq = ((i % tiles_per_seq) * tm + lax.broadcasted_iota(jnp.int32, (tm, 1), 0)).astype(jnp.float32)
    run = ext
    pooled = []
    for g, w in enumerate(POOL_WINDOWS):
        run = run + pltpu.roll(run, w // 2, axis=0)
        cnt = jnp.minimum(t_seq + 1.0, float(w))
        lo, hi = g * POOL_GROUP, (g + 1) * POOL_GROUP
        mean = run[POOL_HALO:, lo:hi] / cnt
        mixed = (mean - ext[POOL_HALO:, lo:hi])
        pooled.append(_mm(mixed, wgrp_ref[g]) * spool_ref[g:g + 1, :])
    pool_out = jnp.concatenate(pooled, axis=1)

    y_attn = jnp.dot(attn_ref[...], wua_ref[...], preferred_element_type=jnp.float32)
    y_pool = _mm(pool_out, wup_ref[...])
    merged = gates[:, 0:D_MODEL] * y_attn + gates[:, D_MODEL:] * y_pool
    x1 = x + gate1 * _mm(merged, wo_ref[...])
    x1_ref[...] = x1

    h2 = (_rms_rows(x1) * g2_ref[...]) * (1.0 + scale2) + shift2
    h2t = h2.T.astype(MXU_DTYPE)
    h2t_ref[...] = h2t
    pqt_ref[...] = jnp.dot(wqt_ref[...], h2t, preferred_element_type=jnp.float32)


def _mix_call(x2, ada3, g1, g2, attn, zp, wg, wgrp, spool, wua, wup, wo, wqt, seq, tm):
    n, d = x2.shape
    tiles_per_seq = seq // tm
    halo_per_tile = tm // POOL_HALO
    kern = functools.partial(_mix_kernel, tiles_per_seq=tiles_per_seq, tm=tm)
    const2 = lambda i: (0, 0)
    const3 = lambda i: (0, 0, 0)
    return pl.pallas_call(
        kern,
        grid=(n // tm,),
        in_specs=[pl.BlockSpec((tm, d), lambda i: (i, 0)),
                  pl.BlockSpec((1, 6, d), lambda i: (i // tiles_per_seq, 0, 0)),
                  pl.BlockSpec((1, d), const2),
                  pl.BlockSpec((1, d), const2),
                  pl.BlockSpec((tm, ATTN_WIDTH), lambda i: (i, 0)),
                  pl.BlockSpec((tm, POOL_WIDTH), lambda i: (i, 0)),
                  pl.BlockSpec((POOL_HALO, POOL_WIDTH), lambda i: (jnp.maximum(i * halo_per_tile - 1, 0), 0)),
                  pl.BlockSpec(wg.shape, const2),
                  pl.BlockSpec(wgrp.shape, const3),
                  pl.BlockSpec(spool.shape, const2),
                  pl.BlockSpec(wua.shape, const2),
                  pl.BlockSpec(wup.shape, const2),
                  pl.BlockSpec(wo.shape, const2),
                  pl.BlockSpec(wqt.shape, const2)],
        out_specs=(pl.BlockSpec((tm, d), lambda i: (i, 0)),
                   pl.BlockSpec((d, tm), lambda i: (0, i)),
                   pl.BlockSpec((d, tm), lambda i: (0, i))),
        out_shape=(jax.ShapeDtypeStruct((n, d), jnp.float32),
                   jax.ShapeDtypeStruct((d, n), MXU_DTYPE),
                   jax.ShapeDtypeStruct((d, n), jnp.float32)),
        compiler_params=_cparams(("parallel",)),
        name="mix",
    )(x2, ada3, g1, g2, attn, zp, zp, wg, wgrp, spool, wua, wup, wo, wqt)


_STAIR = [(r1, r2) for r1 in range(PEER_TOPK) for r2 in range(PEER_TOPK // (r1 + 1))]
_STAIR_ROWS = -(-len(_STAIR) // 8) * 8


def _stair_maps(te):
    r1 = np.full((_STAIR_ROWS, 1), -1.0, np.float32)
    r2 = np.full((_STAIR_ROWS, 1), -1.0, np.float32)
    for row, (a, b) in enumerate(_STAIR):
        r1[row, 0], r2[row, 0] = a, b
    seg = (np.arange(PEER_TOPK, dtype=np.float32)[:, None] == r1[None, :, 0]).astype(np.float32)
    return jnp.asarray(np.tile(r1, (1, te))), jnp.asarray(np.tile(r2, (1, te))), jnp.asarray(seg)


def _extract_top(s, n_rounds, tie_exact):
    r, t = s.shape
    rows = lax.broadcasted_iota(jnp.int32, (r, t), 0).astype(jnp.float32)
    rank = jnp.full((r, t), float(n_rounds), jnp.float32)
    vals = []
    for rnd in range(n_rounds):
        m = jnp.max(s, axis=0, keepdims=True)
        hit = s == m
        if tie_exact:
            first = jnp.min(jnp.where(hit, rows, float(r)), axis=0, keepdims=True)
            hit = rows == first
        rank = jnp.where(hit, float(rnd), rank)
        s = jnp.where(hit, -jnp.inf, s)
        vals.append(m)
    return vals, rank


def _count_rows(mask):
    return jnp.sum(jnp.where(mask, 1.0, 0.0), axis=0, keepdims=True)


def _head_select(sc0, sc1, r1map, r2map, seg, tie_exact):
    kf = float(PEER_TOPK)
    vals0, rk0 = _extract_top(sc0, PEER_TOPK, tie_exact)
    vals1, rk1 = _extract_top(sc1, PEER_TOPK, tie_exact)
    a1 = jnp.full(r1map.shape, -jnp.inf, jnp.float32)
    a2 = jnp.zeros(r2map.shape, jnp.float32)
    for r in range(PEER_TOPK):
        a1 = jnp.where(r1map == float(r), vals0[r], a1)
        a2 = jnp.where(r2map == float(r), vals1[r], a2)
    best, rkc = _extract_top(a1 + a2, PEER_TOPK, tie_exact)
    picked = rkc < kf
    counts = jnp.dot(seg, jnp.where(picked, 1.0, 0.0), preferred_element_type=jnp.float32)
    c_dense = jnp.zeros(sc0.shape, jnp.float32)
    for r in range(PEER_TOPK):
        c_dense = jnp.where(rk0 == float(r), counts[r:r + 1, :], c_dense)
    zsum = jnp.ones_like(best[0])
    for j in range(1, PEER_TOPK):
        zsum = zsum + jnp.exp(best[j] - best[0])
    p1 = jnp.where(rk0 < kf, jnp.exp(sc0 - vals0[0]), 0.0) / zsum
    p2 = jnp.where(rk1 < kf, jnp.exp(sc1 - vals1[0]), 0.0)
    bad = (_count_rows(rk0 < kf) != kf) | (_count_rows(rk1 < kf) != kf) | (_count_rows(picked) != kf)
    return c_dense, p1, rk1, p2, jnp.sum(jnp.where(bad, 1.0, 0.0))


def _peer_select_kernel(pqt_ref, sub_ref, r1map_ref, r2map_ref, seg_ref, c_ref, p1_ref, r2_ref, p2_ref):
    def scores(hh):
        sc = []
        for half in range(2):
            qrows = pqt_ref[pl.ds(pl.multiple_of((hh * 2 + half) * PEER_HALF, PEER_HALF), PEER_HALF), :]
            sc.append(jnp.dot(sub_ref[hh, half], qrows, precision=_HI, preferred_element_type=jnp.float32))
        return sc

    def emit(hh, res):
        c_ref[hh] = res[0]
        p1_ref[hh] = res[1]
        r2_ref[hh] = res[2].astype(r2_ref.dtype)
        p2_ref[hh] = res[3].astype(p2_ref.dtype)

    def head_group_step(i, carry):
        heads = [i * PEER_HEAD_GROUP + k for k in range(PEER_HEAD_GROUP)]
        sc = [scores(hh) for hh in heads]
        fast = [_head_select(s[0], s[1], r1map_ref[...], r2map_ref[...], seg_ref[...], tie_exact=False) for s in sc]
        for hh, res in zip(heads, fast):
            emit(hh, res)
        for hh, s, res in zip(heads, sc, fast):
            @pl.when(res[4] > 0.0)
            def _(hh=hh, s=s):
                emit(hh, _head_select(s[0], s[1], r1map_ref[...], r2map_ref[...], seg_ref[...], tie_exact=True))
        return carry

    lax.fori_loop(0, PEER_HEADS // PEER_HEAD_GROUP, head_group_step, 0)


def _peer_select_call(pqt, sub, te):
    d, n = pqt.shape
    r1map, r2map, seg = _stair_maps(te)
    f32_shape = jax.ShapeDtypeStruct((PEER_HEADS, N_KEYS, n), jnp.float32)
    b16_shape = jax.ShapeDtypeStruct((PEER_HEADS, N_KEYS, n), MXU_DTYPE)
    ospec = pl.BlockSpec((PEER_HEADS, N_KEYS, te), lambda i: (0, 0, i))
    return pl.pallas_call(
        _peer_select_kernel,
        grid=(n // te,),
        in_specs=[pl.BlockSpec((d, te), lambda i: (0, i)),
                  pl.BlockSpec(sub.shape, lambda i: (0, 0, 0, 0)),
                  pl.BlockSpec(r1map.shape, lambda i: (0, 0)),
                  pl.BlockSpec(r2map.shape, lambda i: (0, 0)),
                  pl.BlockSpec(seg.shape, lambda i: (0, 0))],
        out_specs=(ospec, ospec, ospec, ospec),
        out_shape=(f32_shape, f32_shape, b16_shape, b16_shape),
        compiler_params=_cparams(("parallel",)),
        name="peer_select",
    )(pqt, sub, r1map, r2map, seg)


def _peer_dense_kernel(h2t_ref, u_ref, vt_ref, c_ref, p1_ref, r2_ref, p2_ref, x1_ref, ada_ref,
                       o_ref, acc_ref, g_even_ref, g_odd_ref, w_ref, *, keys_per_step, n_chunks):
    j = pl.program_id(1)
    ec = u_ref.shape[0]
    gdt = g_even_ref.dtype

    @pl.when(j == 0)
    def _():
        acc_ref[...] = jnp.zeros_like(acc_ref)
        g_odd_ref[...] = jnp.zeros_like(g_odd_ref)

    first_key = jnp.minimum(j, n_chunks - 1) * keys_per_step

    def step(g_new_ref, g_prev_ref):
        tile = (GATE_ROWS, w_ref.shape[1])
        n_sub = ec // PEER_SUB
        rows_per_sub = acc_ref.shape[0] // n_sub
        for sub in range(n_sub):
            lo = sub * PEER_SUB
            for kk in range(PEER_SUB // N_KEYS):
                i1 = first_key + (lo // N_KEYS + kk)
                w = None
                for hh in range(PEER_HEADS):
                    cb = jnp.broadcast_to(c_ref[hh, pl.ds(i1, 1), :], tile).astype(gdt)[None]
                    pb = jnp.broadcast_to(p1_ref[hh, pl.ds(i1, 1), :], tile).astype(gdt)[None]
                    r2 = r2_ref[hh].reshape(N_KEYS // GATE_ROWS, *tile)
                    p2 = p2_ref[hh].reshape(N_KEYS // GATE_ROWS, *tile)
                    term = jnp.where(r2 < cb, p2, jnp.zeros((), gdt)) * pb
                    w = term if w is None else w + term
                w_ref[lo + kk * N_KEYS:lo + (kk + 1) * N_KEYS, :] = w.reshape(N_KEYS, tile[1])
            rows = slice(sub * rows_per_sub, (sub + 1) * rows_per_sub)
            acc_ref[rows, :] += jnp.dot(vt_ref[rows, :], g_prev_ref[...], preferred_element_type=jnp.float32)
            a = jnp.dot(u_ref[lo:lo + PEER_SUB, :], h2t_ref[...], preferred_element_type=jnp.float32)
            a = a.astype(gdt)
            act = 0.5 * a * (1.0 + lax.erf(a * (2.0 ** -0.5)))
            g_new_ref[lo:lo + PEER_SUB, :] = w_ref[lo:lo + PEER_SUB, :] * act

    pl.when((j % 2 == 0) & (j < n_chunks))(functools.partial(step, g_even_ref, g_odd_ref))
    pl.when((j % 2 == 1) & (j < n_chunks))(functools.partial(step, g_odd_ref, g_even_ref))

    @pl.when(j == n_chunks)
    def _():
        g_last_ref = g_odd_ref if n_chunks % 2 == 0 else g_even_ref
        acc = acc_ref[...] + jnp.dot(vt_ref[...], g_last_ref[...], preferred_element_type=jnp.float32)
        gate2 = ada_ref[0, 5:6, :]
        o_ref[...] = x1_ref[...] + gate2 * acc.T


def _peer_dense_call(h2t, u, vt, c_d, p1_d, r2_d, p2_d, x1, ada3, seq, tt, ec):
    d, n = h2t.shape
    n_chunks = u.shape[0] // ec
    tiles_per_seq = seq // tt
    kern = functools.partial(_peer_dense_kernel, keys_per_step=ec // N_KEYS, n_chunks=n_chunks)
    sel_spec = pl.BlockSpec((PEER_HEADS, N_KEYS, tt), lambda i, j: (0, 0, i))
    return pl.pallas_call(
        kern,
        grid=(n // tt, n_chunks + 1),
        in_specs=[pl.BlockSpec((d, tt), lambda i, j: (0, i)),
                  pl.BlockSpec((ec, d), lambda i, j: (jnp.minimum(j, n_chunks - 1), 0)),
                  pl.BlockSpec((d, ec), lambda i, j: (0, jnp.maximum(j - 1, 0))),
                  sel_spec, sel_spec, sel_spec, sel_spec,
                  pl.BlockSpec((tt, d), lambda i, j: (i, 0)),
                  pl.BlockSpec((1, 6, d), lambda i, j: (i // tiles_per_seq, 0, 0))],
        out_specs=pl.BlockSpec((tt, d), lambda i, j: (i, 0)),
        out_shape=jax.ShapeDtypeStruct((n, d), jnp.float32),
        scratch_shapes=[pltpu.VMEM((d, tt), jnp.float32), pltpu.VMEM((ec, tt), MXU_DTYPE),
                        pltpu.VMEM((ec, tt), MXU_DTYPE), pltpu.VMEM((ec, tt), MXU_DTYPE)],
        compiler_params=_cparams(("parallel", "arbitrary")),
        name="peer_dense",
    )(h2t, u, vt, c_d, p1_d, r2_d, p2_d, x1, ada3)


def _block_diag_mean(width, group):
    idx = np.arange(width) // group
    return jnp.asarray((idx[:, None] == idx[None, :]).astype(np.float32) / group, dtype=jnp.bfloat16)


def _layer(x2, c, w_ada, b_ada, g_norm1, w_in, g_q, g_k, g_ik, w_pool_grp, s_pool, w_up_attn, w_up_pool,
           w_out, g_norm2, w_peer_q, peer_subkeys, peer_u, peer_v, bsz, seq):
    f32 = jnp.float32
    d = D_MODEL
    tm_b = min(512, seq)
    tq = min(512, seq)
    tm_d = min(512, seq)
    te = 256
    tt = min(512, seq)
    ec = 1024

    ada3 = _ada_call(c, w_ada, b_ada).reshape(bsz, 6, d)

    wa = jnp.pad(w_in[:, :_ZP0], ((0, 0), (0, _SLAB_A - _ZP0))).astype(MXU_DTYPE)
    wp = w_in[:, _ZP0:_ZG0].astype(MXU_DTYPE)
    wg = w_in[:, _ZG0:].astype(MXU_DTYPE)
    gcol = jnp.concatenate([jnp.tile(g_q, N_HEADS) * (HEAD_DIM ** -0.5), g_k, jnp.ones((_IK0 - _V0,), f32),
                            g_ik, jnp.ones((_SLAB_A - _IK0 - IDX_DIM,), f32)]).reshape(1, _SLAB_A)
    bd_a = _block_diag_mean(_IQ0, HEAD_DIM)
    bd_b = _block_diag_mean(_SLAB_A - _IK0, IDX_DIM)

    q, k, v, iq, ik, iw, zp = _inproj_call(x2, ada3, g_norm1.reshape(1, d), wa, wp, bd_a, bd_b, gcol, seq, tm_b)

    attn = _dsa_call(q, iq, iw, k, v, ik, bsz, seq, tq)

    x1, h2t, pqt = _mix_call(x2, ada3, g_norm1.reshape(1, d), g_norm2.reshape(1, d), attn, zp, wg,
                             w_pool_grp.astype(MXU_DTYPE), s_pool, w_up_attn.astype(MXU_DTYPE),
                             w_up_pool.astype(MXU_DTYPE), w_out.astype(MXU_DTYPE),
                             w_peer_q.T.astype(MXU_DTYPE), seq, tm_d)

    c_d, p1_d, r2_d, p2_d = _peer_select_call(pqt, peer_subkeys, te)

    return _peer_dense_call(h2t, peer_u.astype(MXU_DTYPE), peer_v.T.astype(MXU_DTYPE),
                            c_d, p1_d, r2_d, p2_d, x1, ada3, seq, tt, ec)


def kernel(x, c, w_ada, b_ada, g_norm1, w_in, g_q, g_k, g_ik, w_pool_grp, s_pool, w_up_attn, w_up_pool, w_out,
           g_norm2, w_peer_q, peer_subkeys, peer_u, peer_v):
    bsz, seq, d = x.shape
    x2 = x.reshape(bsz * seq, d)
    for layer in range(w_ada.shape[0]):
        x2 = _layer(x2, c, w_ada[layer], b_ada[layer], g_norm1[layer], w_in[layer], g_q[layer], g_k[layer],
                    g_ik[layer], w_pool_grp[layer], s_pool[layer], w_up_attn[layer], w_up_pool[layer],
                    w_out[layer], g_norm2[layer], w_peer_q[layer], peer_subkeys[layer], peer_u[layer],
                    peer_v[layer], bsz, seq)
    return x2.reshape(bsz, seq, d)
```

```python
import functools

import jax
import jax.numpy as jnp
import numpy as np
from jax import lax
from jax.experimental import pallas as pl
from jax.experimental.pallas import tpu as pltpu

D_MODEL = 1024
CHUNK = 64
N_HEADS = 8
HEAD_DIM = 64
ATTN_WIDTH = N_HEADS * HEAD_DIM
N_IDX_HEADS = 4
IDX_DIM = 64
TOPK_MAX = 256
DSA_SUB = 256
DSA_STRIP = 64
DSA_PARTS = 2
POOL_WINDOWS = (2, 4, 8, 16)
N_POOL_GROUPS = 4
POOL_WIDTH = 512
POOL_GROUP = POOL_WIDTH // N_POOL_GROUPS
PEER_HEADS = 8
PEER_HALF = 64
N_KEYS = 128
PEER_TOPK = 16
PEER_HEAD_GROUP = 2
GATE_ROWS = 16
PEER_SUB = 256
EPS = 1e-6
POOL_HALO = 16

_Q0, _K0, _V0, _IQ0, _IK0, _IW0 = 0, 512, 576, 640, 896, 960
_SLAB_A = 1024
_ZP0 = 964
_ZG0 = 1476

MXU_DTYPE = jnp.bfloat16
VMEM_LIMIT = 56 * 1024 * 1024

_INT_MIN = -(2 ** 31)
_SOFTMAX_FLOOR = -1e30
_HI = lax.Precision.HIGHEST


def _cparams(sem, flags=None):
    return pltpu.CompilerParams(dimension_semantics=sem, vmem_limit_bytes=VMEM_LIMIT, flags=flags)


def _mm(a, b):
    return jnp.dot(a.astype(MXU_DTYPE), b.astype(MXU_DTYPE), preferred_element_type=jnp.float32)


def _mm_nt(a, b):
    return lax.dot_general(a.astype(MXU_DTYPE), b.astype(MXU_DTYPE), (((1,), (1,)), ((), ())),
                           preferred_element_type=jnp.float32)


def _rms_rows(x):
    return x * lax.rsqrt(jnp.mean(x * x, axis=-1, keepdims=True) + EPS)


def _group_mean_sq(z, bd):
    sq = z * z
    hi = sq.astype(jnp.bfloat16)
    lo = (sq - hi.astype(jnp.float32)).astype(jnp.bfloat16)
    return (jnp.dot(hi, bd, preferred_element_type=jnp.float32)
            + jnp.dot(lo, bd, preferred_element_type=jnp.float32))


def _ada_kernel(c_ref, w_ref, b_ref, o_ref):
    o_ref[...] = jnp.dot(c_ref[...], w_ref[...], precision=_HI,
                         preferred_element_type=jnp.float32) + b_ref[...]


def _ada_call(c, w, b):
    bsz, d = c.shape
    n_out = w.shape[1]
    tn = 1024
    return pl.pallas_call(
        _ada_kernel,
        grid=(n_out // tn,),
        in_specs=[pl.BlockSpec((bsz, d), lambda j: (0, 0)),
                  pl.BlockSpec((d, tn), lambda j: (0, j)),
                  pl.BlockSpec((1, tn), lambda j: (0, j))],
        out_specs=pl.BlockSpec((bsz, tn), lambda j: (0, j)),
        out_shape=jax.ShapeDtypeStruct((bsz, n_out), jnp.float32),
        compiler_params=_cparams(("arbitrary",)),
        name="ada",
    )(c, w, b.reshape(1, n_out))


def _inproj_kernel(x_ref, ada_ref, g1_ref, wa_ref, wp_ref, bd_a_ref, bd_b_ref, gcol_ref,
                   q_ref, k_ref, v_ref, iq_ref, ik_ref, iw_ref, zp_ref):
    x = x_ref[...]
    shift1 = ada_ref[0, 0:1, :]
    scale1 = ada_ref[0, 1:2, :]
    h = (_rms_rows(x) * g1_ref[...]) * (1.0 + scale1) + shift1
    hb = h.astype(MXU_DTYPE)
    z = jnp.dot(hb, wa_ref[...], preferred_element_type=jnp.float32)
    zp_ref[...] = jnp.dot(hb, wp_ref[...], preferred_element_type=jnp.float32)

    za = z[:, 0:_IQ0]
    na = za * lax.rsqrt(_group_mean_sq(za, bd_a_ref[...]) + EPS) * gcol_ref[:, 0:_IQ0]
    for hh in range(N_HEADS):
        q_ref[hh] = na[:, hh * HEAD_DIM:(hh + 1) * HEAD_DIM].astype(q_ref.dtype)
    k_ref[...] = na[:, _K0:_K0 + HEAD_DIM].astype(k_ref.dtype)
    lane = lax.broadcasted_iota(jnp.int32, (z.shape[0], 2 * HEAD_DIM), 1)
    v_ref[...] = jnp.where(lane >= HEAD_DIM, z[:, _K0:_K0 + 2 * HEAD_DIM],
                           jnp.where(lane == 0, 1.0, 0.0)).astype(v_ref.dtype)
    for hh in range(N_IDX_HEADS):
        iq_ref[hh] = (z[:, _IQ0 + hh * IDX_DIM:_IQ0 + (hh + 1) * IDX_DIM] * (IDX_DIM ** -0.5)).astype(iq_ref.dtype)
    zb = z[:, _IK0:_SLAB_A]
    nb = zb * lax.rsqrt(_group_mean_sq(zb, bd_b_ref[...]) + EPS) * gcol_ref[:, _IK0:_SLAB_A]
    ik_ref[...] = nb[:, 0:IDX_DIM].astype(ik_ref.dtype)
    iw_ref[...] = zb[:, IDX_DIM:IDX_DIM + N_IDX_HEADS] * (N_IDX_HEADS ** -0.5)


def _inproj_call(x2, ada3, g1, wa, wp, bd_a, bd_b, gcol, seq, tm):
    n, d = x2.shape
    tiles_per_seq = seq // tm
    f32 = jnp.float32
    out_shape = (
        jax.ShapeDtypeStruct((N_HEADS, n, HEAD_DIM), MXU_DTYPE),
        jax.ShapeDtypeStruct((n, HEAD_DIM), MXU_DTYPE),
        jax.ShapeDtypeStruct((n, 2 * HEAD_DIM), MXU_DTYPE),
        jax.ShapeDtypeStruct((N_IDX_HEADS, n, IDX_DIM), MXU_DTYPE),
        jax.ShapeDtypeStruct((n, IDX_DIM), MXU_DTYPE),
        jax.ShapeDtypeStruct((n, N_IDX_HEADS), f32),
        jax.ShapeDtypeStruct((n, POOL_WIDTH), f32),
    )
    const2 = lambda i: (0, 0)
    return pl.pallas_call(
        _inproj_kernel,
        grid=(n // tm,),
        in_specs=[pl.BlockSpec((tm, d), lambda i: (i, 0)),
                  pl.BlockSpec((1, 6, d), lambda i: (i // tiles_per_seq, 0, 0)),
                  pl.BlockSpec((1, d), const2),
                  pl.BlockSpec(wa.shape, const2),
                  pl.BlockSpec(wp.shape, const2),
                  pl.BlockSpec(bd_a.shape, const2),
                  pl.BlockSpec(bd_b.shape, const2),
                  pl.BlockSpec(gcol.shape, const2)],
        out_specs=(pl.BlockSpec((N_HEADS, tm, HEAD_DIM), lambda i: (0, i, 0)),
                   pl.BlockSpec((tm, HEAD_DIM), lambda i: (i, 0)),
                   pl.BlockSpec((tm, 2 * HEAD_DIM), lambda i: (i, 0)),
                   pl.BlockSpec((N_IDX_HEADS, tm, IDX_DIM), lambda i: (0, i, 0)),
                   pl.BlockSpec((tm, IDX_DIM), lambda i: (i, 0)),
                   pl.BlockSpec((tm, N_IDX_HEADS), lambda i: (i, 0)),
                   pl.BlockSpec((tm, POOL_WIDTH), lambda i: (i, 0))),
        out_shape=out_shape,
        compiler_params=_cparams(("parallel",)),
        name="inproj",
    )(x2, ada3, g1, wa, wp, bd_a, bd_b, gcol)


def _sortable_key(score):
    bits = lax.bitcast_convert_type(score + 0.0, jnp.int32)
    return bits ^ (lax.shift_right_arithmetic(bits, 31) & 0x7FFFFFFF)


def _dsa_kernel(q_ref, iq_ref, iw_ref, k_ref, v_ref, ik_ref, ut_ref, ones_ref, o_ref, key_ref, bias_ref,
                cand_ref, part_ref, *, n_sel, tq):
    blk = pl.program_id(1)
    n_groups = blk + 1
    grp = key_ref.shape[2]
    kf = float(n_sel)
    lanes = grp // 128
    q_chunk = (blk * tq + lax.broadcasted_iota(jnp.int32, (tq, grp), 0)) // CHUNK
    col = lax.broadcasted_iota(jnp.int32, (tq, grp), 1)
    iw = iw_ref[...]

    def build_keys(g, carry):
        ikg = ik_ref[pl.ds(pl.multiple_of(g * grp, grp), grp), :]
        score = None
        for hh in range(N_IDX_HEADS):
            term = jnp.maximum(_mm_nt(iq_ref[hh], ikg), 0.0) * iw[:, hh:hh + 1]
            score = term if score is None else score + term
        key_ref[g] = jnp.where((g * grp + col) // CHUNK <= q_chunk, _sortable_key(score), _INT_MIN)
        return carry

    lax.fori_loop(0, n_groups, build_keys, 0)

    def find_threshold(n):
        part_rows = tq // DSA_PARTS

        def count(part, cand, strictly_greater):
            sums = []
            for st in range(part_rows // DSA_STRIP):
                r0 = part * part_rows + st * DSA_STRIP
                c = jnp.broadcast_to(cand[st * DSA_STRIP:(st + 1) * DSA_STRIP], (DSA_STRIP, 128))
                acc = jnp.zeros((DSA_STRIP, 128), jnp.float32)
                for g in range(n):
                    for t in range(lanes):
                        key = key_ref[g, r0:r0 + DSA_STRIP, t * 128:(t + 1) * 128]
                        acc = acc + jnp.where(key > c if strictly_greater else key >= c, 1.0, 0.0)
                sums.append(acc)
            return jnp.sum(jnp.concatenate(sums, axis=0), axis=1, keepdims=True)

        zero = jnp.zeros((part_rows, 1), jnp.int32)
        start = tuple(jnp.where(count(part, zero, False) >= kf, 0, _INT_MIN).astype(jnp.int32)
                      for part in range(DSA_PARTS))

        def bit_step(i, thrs):
            bit = lax.shift_left(jnp.int32(1), 30 - i)
            return tuple(jnp.where(count(part, thrs[part] | bit, False) >= kf, thrs[part] | bit, thrs[part])
                         for part in range(DSA_PARTS))

        thrs = lax.fori_loop(0, 31, bit_step, start)
        for part in range(DSA_PARTS):
            rows = slice(part * part_rows, (part + 1) * part_rows)
            cand_ref[rows, :] = jnp.broadcast_to(thrs[part], (part_rows, 128))
            part_ref[rows, :] = jnp.broadcast_to(kf - count(part, thrs[part], True), (part_rows, 128))

    for n in range(1, key_ref.shape[0] + 1):
        pl.when(n_groups == n)(functools.partial(find_threshold, n))

    def emit_bias(g, seen):
        thr_t = cand_ref[...]
        room = part_ref[...]
        keys = [key_ref[g, :, t * 128:(t + 1) * 128] for t in range(lanes)]
        eqs = [key == thr_t for key in keys]
        eqb = jnp.concatenate([jnp.where(eq, 1.0, 0.0) for eq in eqs], axis=1).astype(jnp.bfloat16)
        prefix = seen + jnp.dot(eqb, ut_ref[...], preferred_element_type=jnp.float32)
        for t in range(lanes):
            tie_ok = eqs[t] & (prefix[:, t * 128:(t + 1) * 128] <= room)
            sel = ((keys[t] > thr_t) | tie_ok) & (keys[t] != _INT_MIN)
            bias_ref[g, :, t * 128:(t + 1) * 128] = jnp.where(sel, 0.0, -jnp.inf)
        return seen + jnp.dot(eqb, ones_ref[...], preferred_element_type=jnp.float32)

    lax.fori_loop(0, n_groups, emit_bias, jnp.zeros((tq, grp), jnp.float32))

    def sub_block(sub, carry):
        r0 = pl.multiple_of(sub * DSA_SUB, DSA_SUB)
        qs = q_ref[:, pl.ds(r0, DSA_SUB), :].reshape(N_HEADS * DSA_SUB, HEAD_DIM)

        def group_step(g, state):
            m, acc = state
            k0 = pl.multiple_of(g * grp, grp)
            s = _mm_nt(qs, k_ref[pl.ds(k0, grp), :]).reshape(N_HEADS, DSA_SUB, grp)
            s = s + bias_ref[g, pl.ds(r0, DSA_SUB), :][None]
            m_new = jnp.maximum(m, jnp.max(s, axis=2, keepdims=True))
            alpha = jnp.exp(m - m_new)
            p = jnp.exp((s - m_new).astype(MXU_DTYPE))
            pv = jnp.dot(p.reshape(N_HEADS * DSA_SUB, grp), v_ref[pl.ds(k0, grp), :],
                         preferred_element_type=jnp.float32)
            return m_new, alpha * acc + pv.reshape(N_HEADS, DSA_SUB, 2 * HEAD_DIM)

        init = (jnp.full((N_HEADS, DSA_SUB, 1), _SOFTMAX_FLOOR, jnp.float32),
                jnp.zeros((N_HEADS, DSA_SUB, 2 * HEAD_DIM), jnp.float32))
        _, acc = lax.fori_loop(0, n_groups, group_step, init)
        out = acc[:, :, HEAD_DIM:] / acc[:, :, 0:1]
        o_ref[pl.ds(r0, DSA_SUB), :] = jnp.concatenate([out[hh] for hh in range(N_HEADS)], axis=1).astype(o_ref.dtype)
        return carry

    lax.fori_loop(0, tq // DSA_SUB, sub_block, 0)


def _dsa_call(q, iq, iw, k, v, ik, bsz, seq, tq):
    n = k.shape[0]
    n_sel = min(TOPK_MAX, seq // 4)
    nq = seq // tq
    ut = jnp.asarray(np.triu(np.ones((tq, tq), np.float32)), dtype=jnp.bfloat16)
    ones = jnp.ones((tq, tq), jnp.bfloat16)
    kern = functools.partial(_dsa_kernel, n_sel=n_sel, tq=tq)
    per_batch = lambda b, j: (b, 0)
    const2 = lambda b, j: (0, 0)
    return pl.pallas_call(
        kern,
        grid=(bsz, nq),
        in_specs=[pl.BlockSpec((N_HEADS, tq, HEAD_DIM), lambda b, j: (0, b * nq + j, 0)),
                  pl.BlockSpec((N_IDX_HEADS, tq, IDX_DIM), lambda b, j: (0, b * nq + j, 0)),
                  pl.BlockSpec((tq, N_IDX_HEADS), lambda b, j: (b * nq + j, 0)),
                  pl.BlockSpec((seq, HEAD_DIM), per_batch),
                  pl.BlockSpec((seq, 2 * HEAD_DIM), per_batch),
                  pl.BlockSpec((seq, IDX_DIM), per_batch),
                  pl.BlockSpec(ut.shape, const2),
                  pl.BlockSpec(ones.shape, const2)],
        out_specs=pl.BlockSpec((tq, ATTN_WIDTH), lambda b, j: (b * nq + j, 0)),
        out_shape=jax.ShapeDtypeStruct((n, ATTN_WIDTH), MXU_DTYPE),
        scratch_shapes=[pltpu.VMEM((nq, tq, tq), jnp.int32), pltpu.VMEM((nq, tq, tq), jnp.float32),
                        pltpu.VMEM((tq, 128), jnp.int32), pltpu.VMEM((tq, 128), jnp.float32)],
        compiler_params=_cparams(("parallel", "parallel")),
        name="dsa",
    )(q, iq, iw, k, v, ik, ut, ones)


def _mix_kernel(x_ref, ada_ref, g1_ref, g2_ref, attn_ref, zp_ref, halo_ref, wg_ref, wgrp_ref, spool_ref,
                wua_ref, wup_ref, wo_ref, wqt_ref, x1_ref, h2t_ref, pqt_ref, *, tiles_per_seq, tm):
    i = pl.program_id(0)
    x = x_ref[...]
    shift1 = ada_ref[0, 0:1, :]
    scale1 = ada_ref[0, 1:2, :]
    gate1 = ada_ref[0, 2:3, :]
    shift2 = ada_ref[0, 3:4, :]
    scale2 = ada_ref[0, 4:5, :]
    h = (_rms_rows(x) * g1_ref[...]) * (1.0 + scale1) + shift1
    zg = jnp.dot(h.astype(MXU_DTYPE), wg_ref[...], preferred_element_type=jnp.float32)
    gates = jax.nn.sigmoid(zg)

    first_in_seq = (i % tiles_per_seq) == 0
    halo = jnp.where(first_in_seq, 0.0, halo_ref[...])
    ext = jnp.concatenate([halo, zp_ref[...]], axis=0)
    t_seq = ((i % tiles_per_seq) * tm + lax.broadcasted_iota(jnp.int32, (tm, 1), 0)).astype(jnp.float32)
    run = ext
    pooled = []
    for g, w in enumerate(POOL_WINDOWS):
        run = run + pltpu.roll(run, w // 2, axis=0)
        cnt = jnp.minimum(t_seq + 1.0, float(w))
        lo, hi = g * POOL_GROUP, (g + 1) * POOL_GROUP
        mean = run[POOL_HALO:, lo:hi] / cnt
        mixed = (mean - ext[POOL_HALO:, lo:hi])
        pooled.append(_mm(mixed, wgrp_ref[g]) * spool_ref[g:g + 1, :])
    pool_out = jnp.concatenate(pooled, axis=1)

    y_attn = jnp.dot(attn_ref[...], wua_ref[...], preferred_element_type=jnp.float32)
    y_pool = _mm(pool_out, wup_ref[...])
    merged = gates[:, 0:D_MODEL] * y_attn + gates[:, D_MODEL:] * y_pool
    x1 = x + gate1 * _mm(merged, wo_ref[...])
    x1_ref[...] = x1

    h2 = (_rms_rows(x1) * g2_ref[...]) * (1.0 + scale2) + shift2
    h2t = h2.T.astype(MXU_DTYPE)
    h2t_ref[...] = h2t
    pqt_ref[...] = jnp.dot(wqt_ref[...], h2t, preferred_element_type=jnp.float32)


def _mix_call(x2, ada3, g1, g2, attn, zp, wg, wgrp, spool, wua, wup, wo, wqt, seq, tm):
    n, d = x2.shape
    tiles_per_seq = seq // tm
    halo_per_tile = tm // POOL_HALO
    kern = functools.partial(_mix_kernel, tiles_per_seq=tiles_per_seq, tm=tm)
    const2 = lambda i: (0, 0)
    const3 = lambda i: (0, 0, 0)
    return pl.pallas_call(
        kern,
        grid=(n // tm,),
        in_specs=[pl.BlockSpec((tm, d), lambda i: (i, 0)),
                  pl.BlockSpec((1, 6, d), lambda i: (i // tiles_per_seq, 0, 0)),
                  pl.BlockSpec((1, d), const2),
                  pl.BlockSpec((1, d), const2),
                  pl.BlockSpec((tm, ATTN_WIDTH), lambda i: (i, 0)),
                  pl.BlockSpec((tm, POOL_WIDTH), lambda i: (i, 0)),
                  pl.BlockSpec((POOL_HALO, POOL_WIDTH), lambda i: (jnp.maximum(i * halo_per_tile - 1, 0), 0)),
                  pl.BlockSpec(wg.shape, const2),
                  pl.BlockSpec(wgrp.shape, const3),
                  pl.BlockSpec(spool.shape, const2),
                  pl.BlockSpec(wua.shape, const2),
                  pl.BlockSpec(wup.shape, const2),
                  pl.BlockSpec(wo.shape, const2),
                  pl.BlockSpec(wqt.shape, const2)],
        out_specs=(pl.BlockSpec((tm, d), lambda i: (i, 0)),
                   pl.BlockSpec((d, tm), lambda i: (0, i)),
                   pl.BlockSpec((d, tm), lambda i: (0, i))),
        out_shape=(jax.ShapeDtypeStruct((n, d), jnp.float32),
                   jax.ShapeDtypeStruct((d, n), MXU_DTYPE),
                   jax.ShapeDtypeStruct((d, n), jnp.float32)),
        compiler_params=_cparams(("parallel",)),
        name="mix",
    )(x2, ada3, g1, g2, attn, zp, zp, wg, wgrp, spool, wua, wup, wo, wqt)


_STAIR = [(r1, r2) for r1 in range(PEER_TOPK) for r2 in range(PEER_TOPK // (r1 + 1))]
_STAIR_ROWS = -(-len(_STAIR) // 8) * 8


def _stair_maps(te):
    r1 = np.full((_STAIR_ROWS, 1), -1.0, np.float32)
    r2 = np.full((_STAIR_ROWS, 1), -1.0, np.float32)
    for row, (a, b) in enumerate(_STAIR):
        r1[row, 0], r2[row, 0] = a, b
    seg = (np.arange(PEER_TOPK, dtype=np.float32)[:, None] == r1[None, :, 0]).astype(np.float32)
    return jnp.asarray(np.tile(r1, (1, te))), jnp.asarray(np.tile(r2, (1, te))), jnp.asarray(seg)


def _extract_top(s, n_rounds, tie_exact, want_rank=True):
    r, t = s.shape
    rows = lax.broadcasted_iota(jnp.int32, (r, t), 0).astype(jnp.float32)
    rank = jnp.full((r, t), float(n_rounds), jnp.float32) if want_rank else None
    vals = []
    for rnd in range(n_rounds):
        m = jnp.max(s, axis=0, keepdims=True)
        hit = s == m
        if tie_exact:
            first = jnp.min(jnp.where(hit, rows, float(r)), axis=0, keepdims=True)
            hit = rows == first
        if want_rank:
            rank = jnp.where(hit, float(rnd), rank)
        s = jnp.where(hit, -jnp.inf, s)
        vals.append(m)
    return vals, rank


def _count_rows(mask):
    return jnp.sum(jnp.where(mask, 1.0, 0.0), axis=0, keepdims=True)


def _head_select(sc0, sc1, r1map, r2map, seg, tie_exact):
    kf = float(PEER_TOPK)
    vals0, rk0 = _extract_top(sc0, PEER_TOPK, tie_exact, want_rank=tie_exact)
    vals1, rk1 = _extract_top(sc1, PEER_TOPK, tie_exact)
    a1 = jnp.full(r1map.shape, -jnp.inf, jnp.float32)
    a2 = jnp.zeros(r2map.shape, jnp.float32)
    for r in range(PEER_TOPK):
        a1 = jnp.where(r1map == float(r), vals0[r], a1)
        a2 = jnp.where(r2map == float(r), vals1[r], a2)
    cand = a1 + a2
    best, rkc = _extract_top(cand, PEER_TOPK, tie_exact, want_rank=tie_exact)
    if tie_exact:
        member0, picked = rk0 < kf, rkc < kf
        has_rank = lambda r: rk0 == float(r)
    else:
        member0, picked = sc0 >= vals0[PEER_TOPK - 1], cand >= best[PEER_TOPK - 1]
        has_rank = lambda r: sc0 == vals0[r]
    counts = jnp.dot(seg, jnp.where(picked, 1.0, 0.0), preferred_element_type=jnp.float32)
    c_dense = jnp.zeros(sc0.shape, jnp.float32)
    for r in range(PEER_TOPK):
        c_dense = jnp.where(has_rank(r), counts[r:r + 1, :], c_dense)
    zsum = jnp.ones_like(best[0])
    for j in range(1, PEER_TOPK):
        zsum = zsum + jnp.exp(best[j] - best[0])
    p1 = jnp.where(member0, jnp.exp(sc0 - vals0[0]), 0.0) / zsum
    p2 = jnp.where(rk1 < kf, jnp.exp(sc1 - vals1[0]), 0.0)
    bad = (_count_rows(member0) != kf) | (_count_rows(rk1 < kf) != kf) | (_count_rows(picked) != kf)
    return c_dense, p1, rk1, p2, jnp.sum(jnp.where(bad, 1.0, 0.0))


def _peer_select_kernel(pqt_ref, sub_ref, r1map_ref, r2map_ref, seg_ref, c_ref, p1_ref, r2_ref, p2_ref):
    def scores(hh):
        sc = []
        for half in range(2):
            qrows = pqt_ref[pl.ds(pl.multiple_of((hh * 2 + half) * PEER_HALF, PEER_HALF), PEER_HALF), :]
            sc.append(jnp.dot(sub_ref[hh, half], qrows, precision=_HI, preferred_element_type=jnp.float32))
        return sc

    def emit(hh, res):
        c_ref[hh] = res[0]
        p1_ref[hh] = res[1]
        r2_ref[hh] = res[2].astype(r2_ref.dtype)
        p2_ref[hh] = res[3].astype(p2_ref.dtype)

    def head_group_step(i, carry):
        heads = [i * PEER_HEAD_GROUP + k for k in range(PEER_HEAD_GROUP)]
        sc = [scores(hh) for hh in heads]
        fast = [_head_select(s[0], s[1], r1map_ref[...], r2map_ref[...], seg_ref[...], tie_exact=False) for s in sc]
        for hh, res in zip(heads, fast):
            emit(hh, res)
        for hh, s, res in zip(heads, sc, fast):
            @pl.when(res[4] > 0.0)
            def _(hh=hh, s=s):
                emit(hh, _head_select(s[0], s[1], r1map_ref[...], r2map_ref[...], seg_ref[...], tie_exact=True))
        return carry

    lax.fori_loop(0, PEER_HEADS // PEER_HEAD_GROUP, head_group_step, 0)


def _peer_select_call(pqt, sub, te):
    d, n = pqt.shape
    r1map, r2map, seg = _stair_maps(te)
    f32_shape = jax.ShapeDtypeStruct((PEER_HEADS, N_KEYS, n), jnp.float32)
    b16_shape = jax.ShapeDtypeStruct((PEER_HEADS, N_KEYS, n), MXU_DTYPE)
    ospec = pl.BlockSpec((PEER_HEADS, N_KEYS, te), lambda i: (0, 0, i))
    return pl.pallas_call(
        _peer_select_kernel,
        grid=(n // te,),
        in_specs=[pl.BlockSpec((d, te), lambda i: (0, i)),
                  pl.BlockSpec(sub.shape, lambda i: (0, 0, 0, 0)),
                  pl.BlockSpec(r1map.shape, lambda i: (0, 0)),
                  pl.BlockSpec(r2map.shape, lambda i: (0, 0)),
                  pl.BlockSpec(seg.shape, lambda i: (0, 0))],
        out_specs=(ospec, ospec, ospec, ospec),
        out_shape=(f32_shape, f32_shape, b16_shape, b16_shape),
        compiler_params=_cparams(("parallel",)),
        name="peer_select",
    )(pqt, sub, r1map, r2map, seg)


def _peer_dense_kernel(h2t_ref, u_ref, vt_ref, c_ref, p1_ref, r2_ref, p2_ref, x1_ref, ada_ref,
                       o_ref, acc_ref, g_even_ref, g_odd_ref, w_ref, *, keys_per_step, n_chunks):
    j = pl.program_id(1)
    ec = u_ref.shape[0]
    gdt = g_even_ref.dtype

    @pl.when(j == 0)
    def _():
        acc_ref[...] = jnp.zeros_like(acc_ref)
        g_odd_ref[...] = jnp.zeros_like(g_odd_ref)

    first_key = jnp.minimum(j, n_chunks - 1) * keys_per_step

    def step(g_new_ref, g_prev_ref):
        tile = (GATE_ROWS, w_ref.shape[1])
        n_sub = ec // PEER_SUB
        rows_per_sub = acc_ref.shape[0] // n_sub
        for sub in range(n_sub):
            lo = sub * PEER_SUB
            for kk in range(PEER_SUB // N_KEYS):
                i1 = first_key + (lo // N_KEYS + kk)
                w = None
                for hh in range(PEER_HEADS):
                    cb = jnp.broadcast_to(c_ref[hh, pl.ds(i1, 1), :], tile).astype(gdt)[None]
                    pb = jnp.broadcast_to(p1_ref[hh, pl.ds(i1, 1), :], tile).astype(gdt)[None]
                    r2 = r2_ref[hh].reshape(N_KEYS // GATE_ROWS, *tile)
                    p2 = p2_ref[hh].reshape(N_KEYS // GATE_ROWS, *tile)
                    term = jnp.where(r2 < cb, p2, jnp.zeros((), gdt)) * pb
                    w = term if w is None else w + term
                w_ref[lo + kk * N_KEYS:lo + (kk + 1) * N_KEYS, :] = w.reshape(N_KEYS, tile[1])
            rows = slice(sub * rows_per_sub, (sub + 1) * rows_per_sub)
            acc_ref[rows, :] += jnp.dot(vt_ref[rows, :], g_prev_ref[...], preferred_element_type=jnp.float32)
            a = jnp.dot(u_ref[lo:lo + PEER_SUB, :], h2t_ref[...], preferred_element_type=jnp.float32)
            a = a.astype(gdt)
            act = 0.5 * a * (1.0 + lax.erf(a * (2.0 ** -0.5)))
            g_new_ref[lo:lo + PEER_SUB, :] = w_ref[lo:lo + PEER_SUB, :] * act

    pl.when((j % 2 == 0) & (j < n_chunks))(functools.partial(step, g_even_ref, g_odd_ref))
    pl.when((j % 2 == 1) & (j < n_chunks))(functools.partial(step, g_odd_ref, g_even_ref))

    @pl.when(j == n_chunks)
    def _():
        g_last_ref = g_odd_ref if n_chunks % 2 == 0 else g_even_ref
        acc = acc_ref[...] + jnp.dot(vt_ref[...], g_last_ref[...], preferred_element_type=jnp.float32)
        gate2 = ada_ref[0, 5:6, :]
        o_ref[...] = x1_ref[...] + gate2 * acc.T


def _peer_dense_call(h2t, u, vt, c_d, p1_d, r2_d, p2_d, x1, ada3, seq, tt, ec):
    d, n = h2t.shape
    n_chunks = u.shape[0] // ec
    tiles_per_seq = seq // tt
    kern = functools.partial(_peer_dense_kernel, keys_per_step=ec // N_KEYS, n_chunks=n_chunks)
    sel_spec = pl.BlockSpec((PEER_HEADS, N_KEYS, tt), lambda i, j: (0, 0, i))
    return pl.pallas_call(
        kern,
        grid=(n // tt, n_chunks + 1),
        in_specs=[pl.BlockSpec((d, tt), lambda i, j: (0, i)),
                  pl.BlockSpec((ec, d), lambda i, j: (jnp.minimum(j, n_chunks - 1), 0)),
                  pl.BlockSpec((d, ec), lambda i, j: (0, jnp.maximum(j - 1, 0))),
                  sel_spec, sel_spec, sel_spec, sel_spec,
                  pl.BlockSpec((tt, d), lambda i, j: (i, 0)),
                  pl.BlockSpec((1, 6, d), lambda i, j: (i // tiles_per_seq, 0, 0))],
        out_specs=pl.BlockSpec((tt, d), lambda i, j: (i, 0)),
        out_shape=jax.ShapeDtypeStruct((n, d), jnp.float32),
        scratch_shapes=[pltpu.VMEM((d, tt), jnp.float32), pltpu.VMEM((ec, tt), MXU_DTYPE),
                        pltpu.VMEM((ec, tt), MXU_DTYPE), pltpu.VMEM((ec, tt), MXU_DTYPE)],
        compiler_params=_cparams(("parallel", "arbitrary")),
        name="peer_dense",
    )(h2t, u, vt, c_d, p1_d, r2_d, p2_d, x1, ada3)


def _block_diag_mean(width, group):
    idx = np.arange(width) // group
    return jnp.asarray((idx[:, None] == idx[None, :]).astype(np.float32) / group, dtype=jnp.bfloat16)


def _layer(x2, c, w_ada, b_ada, g_norm1, w_in, g_q, g_k, g_ik, w_pool_grp, s_pool, w_up_attn, w_up_pool,
           w_out, g_norm2, w_peer_q, peer_subkeys, peer_u, peer_v, bsz, seq):
    f32 = jnp.float32
    d = D_MODEL
    tm_b = min(512, seq)
    tq = min(512, seq)
    tm_d = min(512, seq)
    te = 256
    tt = min(512, seq)
    ec = 1024

    ada3 = _ada_call(c, w_ada, b_ada).reshape(bsz, 6, d)

    wa = jnp.pad(w_in[:, :_ZP0], ((0, 0), (0, _SLAB_A - _ZP0))).astype(MXU_DTYPE)
    wp = w_in[:, _ZP0:_ZG0].astype(MXU_DTYPE)
    wg = w_in[:, _ZG0:].astype(MXU_DTYPE)
    gcol = jnp.concatenate([jnp.tile(g_q, N_HEADS) * (HEAD_DIM ** -0.5), g_k, jnp.ones((_IK0 - _V0,), f32),
                            g_ik, jnp.ones((_SLAB_A - _IK0 - IDX_DIM,), f32)]).reshape(1, _SLAB_A)
    bd_a = _block_diag_mean(_IQ0, HEAD_DIM)
    bd_b = _block_diag_mean(_SLAB_A - _IK0, IDX_DIM)

    q, k, v, iq, ik, iw, zp = _inproj_call(x2, ada3, g_norm1.reshape(1, d), wa, wp, bd_a, bd_b, gcol, seq, tm_b)

    attn = _dsa_call(q, iq, iw, k, v, ik, bsz, seq, tq)

    x1, h2t, pqt = _mix_call(x2, ada3, g_norm1.reshape(1, d), g_norm2.reshape(1, d), attn, zp, wg,
                             w_pool_grp.astype(MXU_DTYPE), s_pool, w_up_attn.astype(MXU_DTYPE),
                             w_up_pool.astype(MXU_DTYPE), w_out.astype(MXU_DTYPE),
                             w_peer_q.T.astype(MXU_DTYPE), seq, tm_d)

    c_d, p1_d, r2_d, p2_d = _peer_select_call(pqt, peer_subkeys, te)

    return _peer_dense_call(h2t, peer_u.astype(MXU_DTYPE), peer_v.T.astype(MXU_DTYPE),
                            c_d, p1_d, r2_d, p2_d, x1, ada3, seq, tt, ec)


def kernel(x, c, w_ada, b_ada, g_norm1, w_in, g_q, g_k, g_ik, w_pool_grp, s_pool, w_up_attn, w_up_pool, w_out,
           g_norm2, w_peer_q, peer_subkeys, peer_u, peer_v):
    bsz, seq, d = x.shape
    x2 = x.reshape(bsz * seq, d)
    for layer in range(w_ada.shape[0]):
        x2 = _layer(x2, c, w_ada[layer], b_ada[layer], g_norm1[layer], w_in[layer], g_q[layer], g_k[layer],
                    g_ik[layer], w_pool_grp[layer], s_pool[layer], w_up_attn[layer], w_up_pool[layer],
                    w_out[layer], g_norm2[layer], w_peer_q[layer], peer_subkeys[layer], peer_u[layer],
                    peer_v[layer], bsz, seq)
    return x2.reshape(bsz, seq, d)
```

```python
import functools

import jax
import jax.numpy as jnp
import numpy as np
from jax import lax
from jax.experimental import pallas as pl
from jax.experimental.pallas import tpu as pltpu

D_MODEL = 1024
CHUNK = 64
N_HEADS = 8
HEAD_DIM = 64
ATTN_WIDTH = N_HEADS * HEAD_DIM
N_IDX_HEADS = 4
IDX_DIM = 64
TOPK_MAX = 256
DSA_SUB = 256
DSA_STRIP = 64
DSA_PARTS = 2
POOL_WINDOWS = (2, 4, 8, 16)
N_POOL_GROUPS = 4
POOL_WIDTH = 512
POOL_GROUP = POOL_WIDTH // N_POOL_GROUPS
PEER_HEADS = 8
PEER_HALF = 64
N_KEYS = 128
PEER_TOPK = 16
PEER_HEAD_GROUP = 4
GATE_ROWS = 16
PEER_SUB = 256
EPS = 1e-6
POOL_HALO = 16

_Q0, _K0, _V0, _IQ0, _IK0, _IW0 = 0, 512, 576, 640, 896, 960
_SLAB_A = 1024
_ZP0 = 964
_ZG0 = 1476

MXU_DTYPE = jnp.bfloat16
VMEM_LIMIT = 56 * 1024 * 1024

_INT_MIN = -(2 ** 31)
_SOFTMAX_FLOOR = -3.0e38
_HI = lax.Precision.HIGHEST


def _cparams(sem, flags=None):
    return pltpu.CompilerParams(dimension_semantics=sem, vmem_limit_bytes=VMEM_LIMIT, flags=flags)


def _mm(a, b):
    return jnp.dot(a.astype(MXU_DTYPE), b.astype(MXU_DTYPE), preferred_element_type=jnp.float32)


def _mm_nt(a, b):
    return lax.dot_general(a.astype(MXU_DTYPE), b.astype(MXU_DTYPE), (((1,), (1,)), ((), ())),
                           preferred_element_type=jnp.float32)


def _rms_rows(x):
    return x * lax.rsqrt(jnp.mean(x * x, axis=-1, keepdims=True) + EPS)


def _group_mean_sq(z, bd):
    sq = z * z
    hi = sq.astype(jnp.bfloat16)
    lo = (sq - hi.astype(jnp.float32)).astype(jnp.bfloat16)
    return (jnp.dot(hi, bd, preferred_element_type=jnp.float32)
            + jnp.dot(lo, bd, preferred_element_type=jnp.float32))


def _ada_kernel(c_ref, w_ref, b_ref, o_ref):
    o_ref[...] = jnp.dot(c_ref[...], w_ref[...], precision=_HI,
                         preferred_element_type=jnp.float32) + b_ref[...]


def _ada_call(c, w, b):
    bsz, d = c.shape
    n_out = w.shape[1]
    tn = 1024
    return pl.pallas_call(
        _ada_kernel,
        grid=(n_out // tn,),
        in_specs=[pl.BlockSpec((bsz, d), lambda j: (0, 0)),
                  pl.BlockSpec((d, tn), lambda j: (0, j)),
                  pl.BlockSpec((1, tn), lambda j: (0, j))],
        out_specs=pl.BlockSpec((bsz, tn), lambda j: (0, j)),
        out_shape=jax.ShapeDtypeStruct((bsz, n_out), jnp.float32),
        compiler_params=_cparams(("arbitrary",)),
        name="ada",
    )(c, w, b.reshape(1, n_out))


def _inproj_kernel(x_ref, ada_ref, g1_ref, wa_ref, wp_ref, bd_a_ref, bd_b_ref, gcol_ref,
                   q_ref, k_ref, v_ref, iq_ref, ik_ref, iw_ref, zp_ref):
    x = x_ref[...]
    shift1 = ada_ref[0, 0:1, :]
    scale1 = ada_ref[0, 1:2, :]
    h = (_rms_rows(x) * g1_ref[...]) * (1.0 + scale1) + shift1
    hb = h.astype(MXU_DTYPE)
    z = jnp.dot(hb, wa_ref[...], preferred_element_type=jnp.float32)
    zp_ref[...] = jnp.dot(hb, wp_ref[...], preferred_element_type=jnp.float32)

    za = z[:, 0:_IQ0]
    na = za * lax.rsqrt(_group_mean_sq(za, bd_a_ref[...]) + EPS) * gcol_ref[:, 0:_IQ0]
    for hh in range(N_HEADS):
        q_ref[hh] = na[:, hh * HEAD_DIM:(hh + 1) * HEAD_DIM].astype(q_ref.dtype)
    k_ref[...] = na[:, _K0:_K0 + HEAD_DIM].astype(k_ref.dtype)
    lane = lax.broadcasted_iota(jnp.int32, (z.shape[0], 2 * HEAD_DIM), 1)
    v_ref[...] = jnp.where(lane >= HEAD_DIM, z[:, _K0:_K0 + 2 * HEAD_DIM],
                           jnp.where(lane == 0, 1.0, 0.0)).astype(v_ref.dtype)
    for hh in range(N_IDX_HEADS):
        iq_ref[hh] = (z[:, _IQ0 + hh * IDX_DIM:_IQ0 + (hh + 1) * IDX_DIM] * (IDX_DIM ** -0.5)).astype(iq_ref.dtype)
    zb = z[:, _IK0:_SLAB_A]
    nb = zb * lax.rsqrt(_group_mean_sq(zb, bd_b_ref[...]) + EPS) * gcol_ref[:, _IK0:_SLAB_A]
    ik_ref[...] = nb[:, 0:IDX_DIM].astype(ik_ref.dtype)
    iw_ref[...] = zb[:, IDX_DIM:IDX_DIM + N_IDX_HEADS] * (N_IDX_HEADS ** -0.5)


def _inproj_call(x2, ada3, g1, wa, wp, bd_a, bd_b, gcol, seq, tm):
    n, d = x2.shape
    tiles_per_seq = seq // tm
    f32 = jnp.float32
    out_shape = (
        jax.ShapeDtypeStruct((N_HEADS, n, HEAD_DIM), MXU_DTYPE),
        jax.ShapeDtypeStruct((n, HEAD_DIM), MXU_DTYPE),
        jax.ShapeDtypeStruct((n, 2 * HEAD_DIM), MXU_DTYPE),
        jax.ShapeDtypeStruct((N_IDX_HEADS, n, IDX_DIM), MXU_DTYPE),
        jax.ShapeDtypeStruct((n, IDX_DIM), MXU_DTYPE),
        jax.ShapeDtypeStruct((n, N_IDX_HEADS), f32),
        jax.ShapeDtypeStruct((n, POOL_WIDTH), f32),
    )
    const2 = lambda i: (0, 0)
    return pl.pallas_call(
        _inproj_kernel,
        grid=(n // tm,),
        in_specs=[pl.BlockSpec((tm, d), lambda i: (i, 0)),
                  pl.BlockSpec((1, 6, d), lambda i: (i // tiles_per_seq, 0, 0)),
                  pl.BlockSpec((1, d), const2),
                  pl.BlockSpec(wa.shape, const2),
                  pl.BlockSpec(wp.shape, const2),
                  pl.BlockSpec(bd_a.shape, const2),
                  pl.BlockSpec(bd_b.shape, const2),
                  pl.BlockSpec(gcol.shape, const2)],
        out_specs=(pl.BlockSpec((N_HEADS, tm, HEAD_DIM), lambda i: (0, i, 0)),
                   pl.BlockSpec((tm, HEAD_DIM), lambda i: (i, 0)),
                   pl.BlockSpec((tm, 2 * HEAD_DIM), lambda i: (i, 0)),
                   pl.BlockSpec((N_IDX_HEADS, tm, IDX_DIM), lambda i: (0, i, 0)),
                   pl.BlockSpec((tm, IDX_DIM), lambda i: (i, 0)),
                   pl.BlockSpec((tm, N_IDX_HEADS), lambda i: (i, 0)),
                   pl.BlockSpec((tm, POOL_WIDTH), lambda i: (i, 0))),
        out_shape=out_shape,
        compiler_params=_cparams(("parallel",)),
        name="inproj",
    )(x2, ada3, g1, wa, wp, bd_a, bd_b, gcol)


def _sortable_key(score):
    bits = lax.bitcast_convert_type(score + 0.0, jnp.int32)
    return bits ^ (lax.shift_right_arithmetic(bits, 31) & 0x7FFFFFFF)


def _dsa_kernel(q_ref, iq_ref, iw_ref, k_ref, v_ref, ik_ref, ut_ref, ones_ref, o_ref, key_ref, bias_ref,
                cand_ref, part_ref, *, n_sel, tq):
    blk = pl.program_id(1)
    n_groups = blk + 1
    grp = key_ref.shape[2]
    kf = float(n_sel)
    lanes = grp // 128
    q_chunk = (blk * tq + lax.broadcasted_iota(jnp.int32, (tq, grp), 0)) // CHUNK
    col = lax.broadcasted_iota(jnp.int32, (tq, grp), 1)
    iw = iw_ref[...]

    def build_keys(g, carry):
        ikg = ik_ref[pl.ds(pl.multiple_of(g * grp, grp), grp), :]
        score = None
        for hh in range(N_IDX_HEADS):
            term = jnp.maximum(_mm_nt(iq_ref[hh], ikg), 0.0) * iw[:, hh:hh + 1]
            score = term if score is None else score + term
        key_ref[g] = jnp.where((g * grp + col) // CHUNK <= q_chunk, _sortable_key(score), _INT_MIN)
        return carry

    lax.fori_loop(0, n_groups, build_keys, 0)

    def find_threshold(n):
        part_rows = tq // DSA_PARTS

        def count(part, cand, strictly_greater):
            sums = []
            for st in range(part_rows // DSA_STRIP):
                r0 = part * part_rows + st * DSA_STRIP
                c = jnp.broadcast_to(cand[st * DSA_STRIP:(st + 1) * DSA_STRIP], (DSA_STRIP, 128))
                acc = jnp.zeros((DSA_STRIP, 128), jnp.float32)
                for g in range(n):
                    for t in range(lanes):
                        key = key_ref[g, r0:r0 + DSA_STRIP, t * 128:(t + 1) * 128]
                        acc = acc + jnp.where(key > c if strictly_greater else key >= c, 1.0, 0.0)
                sums.append(acc)
            return jnp.sum(jnp.concatenate(sums, axis=0), axis=1, keepdims=True)

        zero = jnp.zeros((part_rows, 1), jnp.int32)
        start = tuple(jnp.where(count(part, zero, False) >= kf, 0, _INT_MIN).astype(jnp.int32)
                      for part in range(DSA_PARTS))

        def bit_step(i, thrs):
            bit = lax.shift_left(jnp.int32(1), 30 - i)
            return tuple(jnp.where(count(part, thrs[part] | bit, False) >= kf, thrs[part] | bit, thrs[part])
                         for part in range(DSA_PARTS))

        thrs = lax.fori_loop(0, 31, bit_step, start)
        for part in range(DSA_PARTS):
            rows = slice(part * part_rows, (part + 1) * part_rows)
            cand_ref[rows, :] = jnp.broadcast_to(thrs[part], (part_rows, 128))
            part_ref[rows, :] = jnp.broadcast_to(kf - count(part, thrs[part], True), (part_rows, 128))

    for n in range(1, key_ref.shape[0] + 1):
        pl.when(n_groups == n)(functools.partial(find_threshold, n))

    def emit_bias(g, seen):
        thr_t = cand_ref[...]
        room = part_ref[...]
        keys = [key_ref[g, :, t * 128:(t + 1) * 128] for t in range(lanes)]
        eqs = [key == thr_t for key in keys]
        eqb = jnp.concatenate([jnp.where(eq, 1.0, 0.0) for eq in eqs], axis=1).astype(jnp.bfloat16)
        prefix = seen + jnp.dot(eqb, ut_ref[...], preferred_element_type=jnp.float32)
        for t in range(lanes):
            tie_ok = eqs[t] & (prefix[:, t * 128:(t + 1) * 128] <= room)
            sel = ((keys[t] > thr_t) | tie_ok) & (keys[t] != _INT_MIN)
            bias_ref[g, :, t * 128:(t + 1) * 128] = jnp.where(sel, 0.0, -jnp.inf)
        return seen + jnp.dot(eqb, ones_ref[...], preferred_element_type=jnp.float32)

    lax.fori_loop(0, n_groups, emit_bias, jnp.zeros((tq, grp), jnp.float32))

    def sub_block(sub, carry):
        r0 = pl.multiple_of(sub * DSA_SUB, DSA_SUB)
        qs = q_ref[:, pl.ds(r0, DSA_SUB), :].reshape(N_HEADS * DSA_SUB, HEAD_DIM)

        def group_step(g, state):
            m, acc = state
            k0 = pl.multiple_of(g * grp, grp)
            s = _mm_nt(qs, k_ref[pl.ds(k0, grp), :]).reshape(N_HEADS, DSA_SUB, grp)
            s = s + bias_ref[g, pl.ds(r0, DSA_SUB), :][None]
            m_new = jnp.maximum(m, jnp.max(s, axis=2, keepdims=True))
            alpha = jnp.exp(m - m_new)
            p = jnp.exp((s - m_new).astype(MXU_DTYPE))
            pv = jnp.dot(p.reshape(N_HEADS * DSA_SUB, grp), v_ref[pl.ds(k0, grp), :],
                         preferred_element_type=jnp.float32)
            return m_new, alpha * acc + pv.reshape(N_HEADS, DSA_SUB, 2 * HEAD_DIM)

        init = (jnp.full((N_HEADS, DSA_SUB, 1), _SOFTMAX_FLOOR, jnp.float32),
                jnp.zeros((N_HEADS, DSA_SUB, 2 * HEAD_DIM), jnp.float32))
        _, acc = lax.fori_loop(0, n_groups, group_step, init)
        out = acc[:, :, HEAD_DIM:] / acc[:, :, 0:1]
        o_ref[pl.ds(r0, DSA_SUB), :] = jnp.concatenate([out[hh] for hh in range(N_HEADS)], axis=1).astype(o_ref.dtype)
        return carry

    lax.fori_loop(0, tq // DSA_SUB, sub_block, 0)


def _dsa_call(q, iq, iw, k, v, ik, bsz, seq, tq):
    n = k.shape[0]
    n_sel = min(TOPK_MAX, seq // 4)
    nq = seq // tq
    ut = jnp.asarray(np.triu(np.ones((tq, tq), np.float32)), dtype=jnp.bfloat16)
    ones = jnp.ones((tq, tq), jnp.bfloat16)
    kern = functools.partial(_dsa_kernel, n_sel=n_sel, tq=tq)
    per_batch = lambda b, j: (b, 0)
    const2 = lambda b, j: (0, 0)
    return pl.pallas_call(
        kern,
        grid=(bsz, nq),
        in_specs=[pl.BlockSpec((N_HEADS, tq, HEAD_DIM), lambda b, j: (0, b * nq + j, 0)),
                  pl.BlockSpec((N_IDX_HEADS, tq, IDX_DIM), lambda b, j: (0, b * nq + j, 0)),
                  pl.BlockSpec((tq, N_IDX_HEADS), lambda b, j: (b * nq + j, 0)),
                  pl.BlockSpec((seq, HEAD_DIM), per_batch),
                  pl.BlockSpec((seq, 2 * HEAD_DIM), per_batch),
                  pl.BlockSpec((seq, IDX_DIM), per_batch),
                  pl.BlockSpec(ut.shape, const2),
                  pl.BlockSpec(ones.shape, const2)],
        out_specs=pl.BlockSpec((tq, ATTN_WIDTH), lambda b, j: (b * nq + j, 0)),
        out_shape=jax.ShapeDtypeStruct((n, ATTN_WIDTH), MXU_DTYPE),
        scratch_shapes=[pltpu.VMEM((nq, tq, tq), jnp.int32), pltpu.VMEM((nq, tq, tq), jnp.float32),
                        pltpu.VMEM((tq, 128), jnp.int32), pltpu.VMEM((tq, 128), jnp.float32)],
        compiler_params=_cparams(("parallel", "parallel")),
        name="dsa",
    )(q, iq, iw, k, v, ik, ut, ones)


def _mix_kernel(x_ref, ada_ref, g1_ref, g2_ref, attn_ref, zp_ref, halo_ref, wg_ref, wgrp_ref, spool_ref,
                wua_ref, wup_ref, wo_ref, wqt_ref, x1_ref, h2t_ref, pqt_ref, *, tiles_per_seq, tm):
    i = pl.program_id(0)
    x = x_ref[...]
    shift1 = ada_ref[0, 0:1, :]
    scale1 = ada_ref[0, 1:2, :]
    gate1 = ada_ref[0, 2:3, :]
    shift2 = ada_ref[0, 3:4, :]
    scale2 = ada_ref[0, 4:5, :]
    h = (_rms_rows(x) * g1_ref[...]) * (1.0 + scale1) + shift1
    zg = jnp.dot(h.astype(MXU_DTYPE), wg_ref[...], preferred_element_type=jnp.float32)
    gates = jax.nn.sigmoid(zg)

    first_in_seq = (i % tiles_per_seq) == 0
    halo = jnp.where(first_in_seq, 0.0, halo_ref[...])
    ext = jnp.concatenate([halo, zp_ref[...]], axis=0)
    t_seq = ((i % tiles_per_seq) * tm + lax.broadcasted_iota(jnp.int32, (tm, 1), 0)).astype(jnp.float32)
    run = ext
    pooled = []
    for g, w in enumerate(POOL_WINDOWS):
        run = run + pltpu.roll(run, w // 2, axis=0)
        cnt = jnp.minimum(t_seq + 1.0, float(w))
        lo, hi = g * POOL_GROUP, (g + 1) * POOL_GROUP
        mean = run[POOL_HALO:, lo:hi] / cnt
        mixed = (mean - ext[POOL_HALO:, lo:hi])
        pooled.append(_mm(mixed, wgrp_ref[g]) * spool_ref[g:g + 1, :])
    pool_out = jnp.concatenate(pooled, axis=1)

    y_attn = jnp.dot(attn_ref[...], wua_ref[...], preferred_element_type=jnp.float32)
    y_pool = _mm(pool_out, wup_ref[...])
    merged = gates[:, 0:D_MODEL] * y_attn + gates[:, D_MODEL:] * y_pool
    x1 = x + gate1 * _mm(merged, wo_ref[...])
    x1_ref[...] = x1

    h2 = (_rms_rows(x1) * g2_ref[...]) * (1.0 + scale2) + shift2
    h2t = h2.T.astype(MXU_DTYPE)
    h2t_ref[...] = h2t
    pqt_ref[...] = jnp.dot(wqt_ref[...], h2t, preferred_element_type=jnp.float32)


def _mix_call(x2, ada3, g1, g2, attn, zp, wg, wgrp, spool, wua, wup, wo, wqt, seq, tm):
    n, d = x2.shape
    tiles_per_seq = seq // tm
    halo_per_tile = tm // POOL_HALO
    kern = functools.partial(_mix_kernel, tiles_per_seq=tiles_per_seq, tm=tm)
    const2 = lambda i: (0, 0)
    const3 = lambda i: (0, 0, 0)
    return pl.pallas_call(
        kern,
        grid=(n // tm,),
        in_specs=[pl.BlockSpec((tm, d), lambda i: (i, 0)),
                  pl.BlockSpec((1, 6, d), lambda i: (i // tiles_per_seq, 0, 0)),
                  pl.BlockSpec((1, d), const2),
                  pl.BlockSpec((1, d), const2),
                  pl.BlockSpec((tm, ATTN_WIDTH), lambda i: (i, 0)),
                  pl.BlockSpec((tm, POOL_WIDTH), lambda i: (i, 0)),
                  pl.BlockSpec((POOL_HALO, POOL_WIDTH), lambda i: (jnp.maximum(i * halo_per_tile - 1, 0), 0)),
                  pl.BlockSpec(wg.shape, const2),
                  pl.BlockSpec(wgrp.shape, const3),
                  pl.BlockSpec(spool.shape, const2),
                  pl.BlockSpec(wua.shape, const2),
                  pl.BlockSpec(wup.shape, const2),
                  pl.BlockSpec(wo.shape, const2),
                  pl.BlockSpec(wqt.shape, const2)],
        out_specs=(pl.BlockSpec((tm, d), lambda i: (i, 0)),
                   pl.BlockSpec((d, tm), lambda i: (0, i)),
                   pl.BlockSpec((d, tm), lambda i: (0, i))),
        out_shape=(jax.ShapeDtypeStruct((n, d), jnp.float32),
                   jax.ShapeDtypeStruct((d, n), MXU_DTYPE),
                   jax.ShapeDtypeStruct((d, n), jnp.float32)),
        compiler_params=_cparams(("parallel",)),
        name="mix",
    )(x2, ada3, g1, g2, attn, zp, zp, wg, wgrp, spool, wua, wup, wo, wqt)


_STAIR = [(r1, r2) for r1 in range(PEER_TOPK) for r2 in range(PEER_TOPK // (r1 + 1))]
_STAIR_ROWS = -(-len(_STAIR) // 8) * 8


def _stair_maps(te):
    r1 = np.full((_STAIR_ROWS, 1), -1.0, np.float32)
    r2 = np.full((_STAIR_ROWS, 1), -1.0, np.float32)
    for row, (a, b) in enumerate(_STAIR):
        r1[row, 0], r2[row, 0] = a, b
    seg = (np.arange(PEER_TOPK, dtype=np.float32)[:, None] == r1[None, :, 0]).astype(np.float32)
    return jnp.asarray(np.tile(r1, (1, te))), jnp.asarray(np.tile(r2, (1, te))), jnp.asarray(seg)


def _extract_top(s, n_rounds, tie_exact, want_rank=True):
    r, t = s.shape
    rows = lax.broadcasted_iota(jnp.int32, (r, t), 0).astype(jnp.float32)
    rank = jnp.full((r, t), float(n_rounds), jnp.float32) if want_rank else None
    vals = []
    for rnd in range(n_rounds):
        m = jnp.max(s, axis=0, keepdims=True)
        hit = s == m
        if tie_exact:
            first = jnp.min(jnp.where(hit, rows, float(r)), axis=0, keepdims=True)
            hit = rows == first
        if want_rank:
            rank = jnp.where(hit, float(rnd), rank)
        s = jnp.where(hit, -jnp.inf, s)
        vals.append(m)
    return vals, rank


def _count_rows(mask):
    return jnp.sum(jnp.where(mask, 1.0, 0.0), axis=0, keepdims=True)


def _head_select(sc0, sc1, r1map, r2map, seg, tie_exact):
    kf = float(PEER_TOPK)
    vals0, rk0 = _extract_top(sc0, PEER_TOPK, tie_exact, want_rank=tie_exact)
    vals1, rk1 = _extract_top(sc1, PEER_TOPK, tie_exact)
    a1 = jnp.full(r1map.shape, -jnp.inf, jnp.float32)
    a2 = jnp.zeros(r2map.shape, jnp.float32)
    for r in range(PEER_TOPK):
        a1 = jnp.where(r1map == float(r), vals0[r], a1)
        a2 = jnp.where(r2map == float(r), vals1[r], a2)
    cand = a1 + a2
    best, rkc = _extract_top(cand, PEER_TOPK, tie_exact, want_rank=tie_exact)
    if tie_exact:
        member0, picked = rk0 < kf, rkc < kf
        has_rank = lambda r: rk0 == float(r)
    else:
        member0, picked = sc0 >= vals0[PEER_TOPK - 1], cand >= best[PEER_TOPK - 1]
        has_rank = lambda r: sc0 == vals0[r]
    counts = jnp.dot(seg, jnp.where(picked, 1.0, 0.0), preferred_element_type=jnp.float32)
    c_dense = jnp.zeros(sc0.shape, jnp.float32)
    for r in range(PEER_TOPK):
        c_dense = jnp.where(has_rank(r), counts[r:r + 1, :], c_dense)
    zsum = jnp.ones_like(best[0])
    for j in range(1, PEER_TOPK):
        zsum = zsum + jnp.exp(best[j] - best[0])
    p1 = jnp.where(member0, jnp.exp(sc0 - vals0[0]), 0.0) / zsum
    p2 = jnp.where(rk1 < kf, jnp.exp(sc1 - vals1[0]), 0.0)
    bad = (_count_rows(member0) != kf) | (_count_rows(rk1 < kf) != kf) | (_count_rows(picked) != kf)
    return c_dense, p1, rk1, p2, jnp.sum(jnp.where(bad, 1.0, 0.0))


def _peer_select_kernel(pqt_ref, sub_ref, r1map_ref, r2map_ref, seg_ref, c_ref, p1_ref, r2_ref, p2_ref):
    def scores(hh):
        sc = []
        for half in range(2):
            qrows = pqt_ref[pl.ds(pl.multiple_of((hh * 2 + half) * PEER_HALF, PEER_HALF), PEER_HALF), :]
            sc.append(jnp.dot(sub_ref[hh, half], qrows, precision=_HI, preferred_element_type=jnp.float32))
        return sc

    def emit(hh, res):
        c_ref[hh] = res[0]
        p1_ref[hh] = res[1]
        r2_ref[hh] = res[2].astype(r2_ref.dtype)
        p2_ref[hh] = res[3].astype(p2_ref.dtype)

    def head_group_step(i, carry):
        heads = [i * PEER_HEAD_GROUP + k for k in range(PEER_HEAD_GROUP)]
        sc = [scores(hh) for hh in heads]
        fast = [_head_select(s[0], s[1], r1map_ref[...], r2map_ref[...], seg_ref[...], tie_exact=False) for s in sc]
        for hh, res in zip(heads, fast):
            emit(hh, res)
        for hh, s, res in zip(heads, sc, fast):
            @pl.when(res[4] > 0.0)
            def _(hh=hh, s=s):
                emit(hh, _head_select(s[0], s[1], r1map_ref[...], r2map_ref[...], seg_ref[...], tie_exact=True))
        return carry

    lax.fori_loop(0, PEER_HEADS // PEER_HEAD_GROUP, head_group_step, 0)


def _peer_select_call(pqt, sub, te):
    d, n = pqt.shape
    r1map, r2map, seg = _stair_maps(te)
    f32_shape = jax.ShapeDtypeStruct((PEER_HEADS, N_KEYS, n), jnp.float32)
    b16_shape = jax.ShapeDtypeStruct((PEER_HEADS, N_KEYS, n), MXU_DTYPE)
    ospec = pl.BlockSpec((PEER_HEADS, N_KEYS, te), lambda i: (0, 0, i))
    return pl.pallas_call(
        _peer_select_kernel,
        grid=(n // te,),
        in_specs=[pl.BlockSpec((d, te), lambda i: (0, i)),
                  pl.BlockSpec(sub.shape, lambda i: (0, 0, 0, 0)),
                  pl.BlockSpec(r1map.shape, lambda i: (0, 0)),
                  pl.BlockSpec(r2map.shape, lambda i: (0, 0)),
                  pl.BlockSpec(seg.shape, lambda i: (0, 0))],
        out_specs=(ospec, ospec, ospec, ospec),
        out_shape=(f32_shape, f32_shape, b16_shape, b16_shape),
        compiler_params=_cparams(("parallel",)),
        name="peer_select",
    )(pqt, sub, r1map, r2map, seg)


def _peer_dense_kernel(h2t_ref, u_ref, vt_ref, c_ref, p1_ref, r2_ref, p2_ref, x1_ref, ada_ref,
                       o_ref, acc_ref, g_even_ref, g_odd_ref, w_ref, *, keys_per_step, n_chunks):
    j = pl.program_id(1)
    ec = u_ref.shape[0]
    gdt = g_even_ref.dtype

    @pl.when(j == 0)
    def _():
        acc_ref[...] = jnp.zeros_like(acc_ref)
        g_odd_ref[...] = jnp.zeros_like(g_odd_ref)

    first_key = jnp.minimum(j, n_chunks - 1) * keys_per_step

    def step(g_new_ref, g_prev_ref):
        tile = (GATE_ROWS, w_ref.shape[1])
        n_sub = ec // PEER_SUB
        rows_per_sub = acc_ref.shape[0] // n_sub
        for sub in range(n_sub):
            lo = sub * PEER_SUB
            for kk in range(PEER_SUB // N_KEYS):
                i1 = first_key + (lo // N_KEYS + kk)
                w = None
                for hh in range(PEER_HEADS):
                    cb = jnp.broadcast_to(c_ref[hh, pl.ds(i1, 1), :], tile).astype(gdt)[None]
                    pb = jnp.broadcast_to(p1_ref[hh, pl.ds(i1, 1), :], tile).astype(gdt)[None]
                    r2 = r2_ref[hh].reshape(N_KEYS // GATE_ROWS, *tile)
                    p2 = p2_ref[hh].reshape(N_KEYS // GATE_ROWS, *tile)
                    term = jnp.where(r2 < cb, p2, jnp.zeros((), gdt)) * pb
                    w = term if w is None else w + term
                w_ref[lo + kk * N_KEYS:lo + (kk + 1) * N_KEYS, :] = w.reshape(N_KEYS, tile[1])
            rows = slice(sub * rows_per_sub, (sub + 1) * rows_per_sub)
            acc_ref[rows, :] += jnp.dot(vt_ref[rows, :], g_prev_ref[...], preferred_element_type=jnp.float32)
            a = jnp.dot(u_ref[lo:lo + PEER_SUB, :], h2t_ref[...], preferred_element_type=jnp.float32)
            a = a.astype(gdt)
            act = 0.5 * a * (1.0 + lax.erf(a * (2.0 ** -0.5)))
            g_new_ref[lo:lo + PEER_SUB, :] = w_ref[lo:lo + PEER_SUB, :] * act

    pl.when((j % 2 == 0) & (j < n_chunks))(functools.partial(step, g_even_ref, g_odd_ref))
    pl.when((j % 2 == 1) & (j < n_chunks))(functools.partial(step, g_odd_ref, g_even_ref))

    @pl.when(j == n_chunks)
    def _():
        g_last_ref = g_odd_ref if n_chunks % 2 == 0 else g_even_ref
        acc = acc_ref[...] + jnp.dot(vt_ref[...], g_last_ref[...], preferred_element_type=jnp.float32)
        gate2 = ada_ref[0, 5:6, :]
        o_ref[...] = x1_ref[...] + gate2 * acc.T


def _peer_dense_call(h2t, u, vt, c_d, p1_d, r2_d, p2_d, x1, ada3, seq, tt, ec):
    d, n = h2t.shape
    n_chunks = u.shape[0] // ec
    tiles_per_seq = seq // tt
    kern = functools.partial(_peer_dense_kernel, keys_per_step=ec // N_KEYS, n_chunks=n_chunks)
    sel_spec = pl.BlockSpec((PEER_HEADS, N_KEYS, tt), lambda i, j: (0, 0, i))
    return pl.pallas_call(
        kern,
        grid=(n // tt, n_chunks + 1),
        in_specs=[pl.BlockSpec((d, tt), lambda i, j: (0, i)),
                  pl.BlockSpec((ec, d), lambda i, j: (jnp.minimum(j, n_chunks - 1), 0)),
                  pl.BlockSpec((d, ec), lambda i, j: (0, jnp.maximum(j - 1, 0))),
                  sel_spec, sel_spec, sel_spec, sel_spec,
                  pl.BlockSpec((tt, d), lambda i, j: (i, 0)),
                  pl.BlockSpec((1, 6, d), lambda i, j: (i // tiles_per_seq, 0, 0))],
        out_specs=pl.BlockSpec((tt, d), lambda i, j: (i, 0)),
        out_shape=jax.ShapeDtypeStruct((n, d), jnp.float32),
        scratch_shapes=[pltpu.VMEM((d, tt), jnp.float32), pltpu.VMEM((ec, tt), MXU_DTYPE),
                        pltpu.VMEM((ec, tt), MXU_DTYPE), pltpu.VMEM((ec, tt), MXU_DTYPE)],
        compiler_params=_cparams(("parallel", "arbitrary")),
        name="peer_dense",
    )(h2t, u, vt, c_d, p1_d, r2_d, p2_d, x1, ada3)


def _block_diag_mean(width, group):
    idx = np.arange(width) // group
    return jnp.asarray((idx[:, None] == idx[None, :]).astype(np.float32) / group, dtype=jnp.bfloat16)


def _layer(x2, c, w_ada, b_ada, g_norm1, w_in, g_q, g_k, g_ik, w_pool_grp, s_pool, w_up_attn, w_up_pool,
           w_out, g_norm2, w_peer_q, peer_subkeys, peer_u, peer_v, bsz, seq):
    f32 = jnp.float32
    d = D_MODEL
    tm_b = min(512, seq)
    tq = min(512, seq)
    tm_d = min(512, seq)
    te = 256
    tt = min(512, seq)
    ec = 1024

    ada3 = _ada_call(c, w_ada, b_ada).reshape(bsz, 6, d)

    wa = jnp.pad(w_in[:, :_ZP0], ((0, 0), (0, _SLAB_A - _ZP0))).astype(MXU_DTYPE)
    wp = w_in[:, _ZP0:_ZG0].astype(MXU_DTYPE)
    wg = w_in[:, _ZG0:].astype(MXU_DTYPE)
    gcol = jnp.concatenate([jnp.tile(g_q, N_HEADS) * (HEAD_DIM ** -0.5), g_k, jnp.ones((_IK0 - _V0,), f32),
                            g_ik, jnp.ones((_SLAB_A - _IK0 - IDX_DIM,), f32)]).reshape(1, _SLAB_A)
    bd_a = _block_diag_mean(_IQ0, HEAD_DIM)
    bd_b = _block_diag_mean(_SLAB_A - _IK0, IDX_DIM)

    q, k, v, iq, ik, iw, zp = _inproj_call(x2, ada3, g_norm1.reshape(1, d), wa, wp, bd_a, bd_b, gcol, seq, tm_b)

    attn = _dsa_call(q, iq, iw, k, v, ik, bsz, seq, tq)

    x1, h2t, pqt = _mix_call(x2, ada3, g_norm1.reshape(1, d), g_norm2.reshape(1, d), attn, zp, wg,
                             w_pool_grp.astype(MXU_DTYPE), s_pool, w_up_attn.astype(MXU_DTYPE),
                             w_up_pool.astype(MXU_DTYPE), w_out.astype(MXU_DTYPE),
                             w_peer_q.T.astype(MXU_DTYPE), seq, tm_d)

    c_d, p1_d, r2_d, p2_d = _peer_select_call(pqt, peer_subkeys, te)

    return _peer_dense_call(h2t, peer_u.astype(MXU_DTYPE), peer_v.T.astype(MXU_DTYPE),
                            c_d, p1_d, r2_d, p2_d, x1, ada3, seq, tt, ec)


def kernel(x, c, w_ada, b_ada, g_norm1, w_in, g_q, g_k, g_ik, w_pool_grp, s_pool, w_up_attn, w_up_pool, w_out,
           g_norm2, w_peer_q, peer_subkeys, peer_u, peer_v):
    bsz, seq, d = x.shape
    x2 = x.reshape(bsz * seq, d)
    for layer in range(w_ada.shape[0]):
        x2 = _layer(x2, c, w_ada[layer], b_ada[layer], g_norm1[layer], w_in[layer], g_q[layer], g_k[layer],
                    g_ik[layer], w_pool_grp[layer], s_pool[layer], w_up_attn[layer], w_up_pool[layer],
                    w_out[layer], g_norm2[layer], w_peer_q[layer], peer_subkeys[layer], peer_u[layer],
                    peer_v[layer], bsz, seq)
    return x2.reshape(bsz, seq, d)
```
